```python
import math
import jax, jax.numpy as jnp
from jax import lax
import numpy as np

D_MODEL = 4096
BATCH = 2
SEQ = 4096
DEPTH = 2

N_META = 16
ATTN_HEADS = 16
HEAD_DIM = 128
D_ATTN = ATTN_HEADS * HEAD_DIM
D_CONV = D_MODEL - D_ATTN
CONV_GROUPS = 16
CONV_WIDTH = 3
D_MIX = D_ATTN + D_CONV
D_IN = 4 * D_ATTN + 4 * D_CONV
Q_BLOCK = 128
EPS = 1e-6

kernel_name = "hymba_stickbreak_shortconv_hybrid"


def rmsnorm(x, g):
    xf = x.astype(jnp.float32)
    y = xf * lax.rsqrt(jnp.mean(xf * xf, axis=-1, keepdims=True) + EPS)
    return (y * g.astype(jnp.float32)).astype(x.dtype)


def group_rmsnorm(x, g, groups):
    shp = x.shape
    xf = x.astype(jnp.float32).reshape(shp[:-1] + (groups, shp[-1] // groups))
    y = xf * lax.rsqrt(jnp.mean(xf * xf, axis=-1, keepdims=True) + EPS)
    return (y.reshape(shp) * g.astype(jnp.float32)).astype(x.dtype)


def stick_breaking_attention(q, k, v):
    L = q.shape[2]
    scale = 1.0 / math.sqrt(HEAD_DIM)
    bounds = [(0, min(N_META, L))] + [(s, min(s + Q_BLOCK, L)) for s in range(N_META, L, Q_BLOCK)]
    outs = []
    for q0, q1 in bounds:
        qb = q[:, :, q0:q1].astype(jnp.float32)
        kb = k[:, :, :q1].astype(jnp.float32)
        vb = v[:, :, :q1].astype(jnp.float32)
        z = jnp.einsum('bhqd,bhkd->bhqk', qb, kb) * scale
        t_pos = jnp.arange(q0, q1)[:, None]
        s_pos = jnp.arange(q1)[None, :]
        mask = s_pos < t_pos
        log_beta = jax.nn.log_sigmoid(z)
        log_keep = jnp.where(mask, jax.nn.log_sigmoid(-z), 0.0)
        after = lax.cumsum(log_keep, axis=3, reverse=True) - log_keep
        a = jnp.where(mask, jnp.exp(log_beta + after), 0.0)
        outs.append(jnp.einsum('bhqk,bhkd->bhqd', a, vb))
    return jnp.concatenate(outs, axis=2).astype(v.dtype)


def short_conv(u, w):
    L = u.shape[1]
    up = jnp.pad(u, ((0, 0), (CONV_WIDTH - 1, 0), (0, 0)))
    y = up[:, 0:L] * w[0]
    for i in range(1, CONV_WIDTH):
        y = y + up[:, i:i + L] * w[i]
    return y


def hybrid_layer(x, norm_g, w_in, conv_w, attn_norm_g, conv_norm_g, w_out):
    B, L, _ = x.shape
    h = rmsnorm(x, norm_g)
    p = h @ w_in
    cuts = [D_ATTN, 2 * D_ATTN, 3 * D_ATTN, 4 * D_ATTN,
            4 * D_ATTN + D_CONV, 4 * D_ATTN + 2 * D_CONV, 4 * D_ATTN + 3 * D_CONV]
    q, k, v, g_attn, b_conv, c_conv, h_conv, z_conv = jnp.split(p, cuts, axis=-1)

    def heads(t):
        return t.reshape(B, L, ATTN_HEADS, HEAD_DIM).transpose(0, 2, 1, 3)
    o = stick_breaking_attention(heads(q), heads(k), heads(v))
    o = o.transpose(0, 2, 1, 3).reshape(B, L, D_ATTN)
    o = group_rmsnorm(o * jax.nn.silu(g_attn), attn_norm_g, ATTN_HEADS)

    y = b_conv * short_conv(c_conv * h_conv, conv_w)
    y = group_rmsnorm(y * jax.nn.silu(z_conv), conv_norm_g, CONV_GROUPS)

    return x + jnp.concatenate([o, y], axis=-1) @ w_out


def setup_inputs(seed: int = 0) -> dict:
    key = jax.random.key(seed)
    ks = jax.random.split(key, 10)
    f32 = jnp.float32
    x = jax.random.normal(ks[0], (BATCH, SEQ, D_MODEL), f32)
    meta_tokens = jax.random.normal(ks[1], (N_META, D_MODEL), f32)
    norm_g = 1.0 + 0.01 * jax.random.normal(ks[2], (DEPTH, D_MODEL), f32)
    w_in = jax.random.normal(ks[3], (DEPTH, D_MODEL, D_IN), f32) * (D_MODEL ** -0.5)
    conv_w = jax.random.normal(ks[4], (DEPTH, CONV_WIDTH, D_CONV), f32) * (CONV_WIDTH ** -0.5)
    attn_norm_g = 1.0 + 0.01 * jax.random.normal(ks[5], (DEPTH, D_ATTN), f32)
    conv_norm_g = 1.0 + 0.01 * jax.random.normal(ks[6], (DEPTH, D_CONV), f32)
    w_out = jax.random.normal(ks[7], (DEPTH, D_MIX, D_MODEL), f32) * (D_MIX ** -0.5)
    final_norm_g = 1.0 + 0.01 * jax.random.normal(ks[8], (D_MODEL,), f32)
    return {"x": x, "meta_tokens": meta_tokens, "norm_g": norm_g, "w_in": w_in,
            "conv_w": conv_w, "attn_norm_g": attn_norm_g, "conv_norm_g": conv_norm_g,
            "w_out": w_out, "final_norm_g": final_norm_g}


def reference(x, meta_tokens, norm_g, w_in, conv_w, attn_norm_g, conv_norm_g, w_out, final_norm_g):
    B = x.shape[0]
    meta = jnp.broadcast_to(meta_tokens.astype(x.dtype)[None], (B, N_META, D_MODEL))
    hs = jnp.concatenate([meta, x], axis=1)
    for l in range(DEPTH):
        hs = hybrid_layer(hs, norm_g[l], w_in[l], conv_w[l], attn_norm_g[l],
                          conv_norm_g[l], w_out[l])
    return rmsnorm(hs, final_norm_g)[:, N_META:]
```

```python
import functools
import math

import jax
import jax.numpy as jnp
from jax import lax
from jax.experimental import pallas as pl
from jax.experimental.pallas import tpu as pltpu

D_MODEL = 4096
SEQ = 4096
N_META = 16
HEAD_DIM = 128
ATTN_HEADS = 16
D_ATTN = ATTN_HEADS * HEAD_DIM
D_CONV = D_MODEL - D_ATTN
CONV_WIDTH = 3
EPS = 1e-6

LANES = 128
META_BLOCK = 128
L_PAD = META_BLOCK + SEQ
KV_BLOCK = 256
N_KV_BLOCKS = SEQ // KV_BLOCK
PROJ_TN = 256
OUT_TN = 512
NORM_TM = 256
VMEM_LIMIT = 56 * 1024 * 1024

F32 = jnp.float32
BF16 = jnp.bfloat16


def _row_tile(m):
    for t in (1056, 768, 704, 528, 384, 256, 128):
        if m % t == 0:
            return t
    raise ValueError(f"unsupported row count {m}")


def _rmsnorm_kernel(x_ref, g_ref, o_ref):
    x = x_ref[...]
    ms = jnp.sum(x * x, axis=-1, keepdims=True) * (1.0 / D_MODEL)
    o_ref[...] = (x * lax.rsqrt(ms + EPS) * g_ref[...]).astype(o_ref.dtype)


def _rmsnorm(x2d, g, out_dtype, tm):
    m = x2d.shape[0]
    return pl.pallas_call(
        _rmsnorm_kernel,
        grid=(m // tm,),
        in_specs=[pl.BlockSpec((tm, D_MODEL), lambda i: (i, 0)),
                  pl.BlockSpec((1, D_MODEL), lambda i: (0, 0))],
        out_specs=pl.BlockSpec((tm, D_MODEL), lambda i: (i, 0)),
        out_shape=jax.ShapeDtypeStruct((m, D_MODEL), out_dtype),
        name="rmsnorm",
        compiler_params=pltpu.CompilerParams(
            dimension_semantics=("arbitrary",), vmem_limit_bytes=VMEM_LIMIT),
    )(x2d, g.reshape(1, D_MODEL))


def _final_norm_kernel(x_ref, g_ref, o_ref):
    x = x_ref[0]
    ms = jnp.sum(x * x, axis=-1, keepdims=True) * (1.0 / D_MODEL)
    o_ref[0] = x * lax.rsqrt(ms + EPS) * g_ref[...]


def _final_norm(hs3d, g):
    b = hs3d.shape[0]
    off = META_BLOCK // 128
    return pl.pallas_call(
        _final_norm_kernel,
        grid=(b, SEQ // 128),
        in_specs=[pl.BlockSpec((1, 128, D_MODEL), lambda bi, i: (bi, i + off, 0)),
                  pl.BlockSpec((1, D_MODEL), lambda bi, i: (0, 0))],
        out_specs=pl.BlockSpec((1, 128, D_MODEL), lambda bi, i: (bi, i, 0)),
        out_shape=jax.ShapeDtypeStruct((b, SEQ, D_MODEL), F32),
        name="final_norm",
        compiler_params=pltpu.CompilerParams(
            dimension_semantics=("arbitrary", "arbitrary"), vmem_limit_bytes=VMEM_LIMIT),
    )(hs3d, g.reshape(1, D_MODEL))


def _proj_attn_kernel(h_ref, wq_ref, wk_ref, wv_ref, wg_ref, q_ref, k_ref, v_ref, g_ref):
    h = h_ref[...]
    q_ref[...] = jnp.dot(h, wq_ref[...], preferred_element_type=F32).astype(BF16)
    k_ref[...] = jnp.dot(h, wk_ref[...], preferred_element_type=F32).astype(BF16)
    v_ref[...] = jnp.dot(h, wv_ref[...], preferred_element_type=F32).astype(BF16)
    g_ref[...] = jnp.dot(h, wg_ref[...], preferred_element_type=F32)


def _proj_attn(h, w_in_l, tm):
    m = h.shape[0]
    nj = D_ATTN // PROJ_TN

    def wspec(seg):
        return pl.BlockSpec((D_MODEL, PROJ_TN), lambda j, i, seg=seg: (0, seg * nj + j))

    ospec = pl.BlockSpec((tm, PROJ_TN), lambda j, i: (i, j))
    return pl.pallas_call(
        _proj_attn_kernel,
        grid=(nj, m // tm),
        in_specs=[pl.BlockSpec((tm, D_MODEL), lambda j, i: (i, 0)),
                  wspec(0), wspec(1), wspec(2), wspec(3)],
        out_specs=[ospec, ospec, ospec, ospec],
        out_shape=[jax.ShapeDtypeStruct((m, D_ATTN), BF16)] * 3
        + [jax.ShapeDtypeStruct((m, D_ATTN), F32)],
        name="proj_attn",
        compiler_params=pltpu.CompilerParams(
            dimension_semantics=("arbitrary", "arbitrary"), vmem_limit_bytes=VMEM_LIMIT),
    )(h, w_in_l, w_in_l, w_in_l, w_in_l)


def _proj_conv_kernel(h_ref, wb_ref, wc_ref, wh_ref, wz_ref, cw_ref, gain_ref, y_ref,
                      u_ref):
    i = pl.program_id(1)
    tm = h_ref.shape[0]
    h = h_ref[...]

    @pl.when(i == 0)
    def _():
        u_ref[0:8, :] = jnp.zeros((8, PROJ_TN), F32)

    @pl.when(i > 0)
    def _():
        u_ref[0:8, :] = u_ref[tm:tm + 8, :]

    u_ref[8:8 + tm, :] = (jnp.dot(h, wc_ref[...], preferred_element_type=F32)
                          * jnp.dot(h, wh_ref[...], preferred_element_type=F32))
    cw = cw_ref[...]
    conv = u_ref[6:6 + tm, :] * cw[0:1, :]
    conv = conv + u_ref[7:7 + tm, :] * cw[1:2, :]
    conv = conv + u_ref[8:8 + tm, :] * cw[2:3, :]
    y = jnp.dot(h, wb_ref[...], preferred_element_type=F32) * conv
    z = jnp.dot(h, wz_ref[...], preferred_element_type=F32)
    y = y * (z * jax.nn.sigmoid(z))
    gain = gain_ref[...]
    for c in range(PROJ_TN // LANES):
        sl = slice(c * LANES, (c + 1) * LANES)
        yc = y[:, sl]
        ms = jnp.sum(yc * yc, axis=-1, keepdims=True) * (1.0 / LANES)
        y_ref[:, sl] = (yc * lax.rsqrt(ms + EPS) * gain[:, sl]).astype(BF16)


def _proj_conv(h, w_in_l, conv_w_l, conv_gain_l, tm):
    m = h.shape[0]
    nj = D_CONV // PROJ_TN
    base = 4 * D_ATTN // PROJ_TN

    def wspec(seg):
        return pl.BlockSpec((D_MODEL, PROJ_TN),
                            lambda j, i, seg=seg: (0, base + seg * nj + j))

    return pl.pallas_call(
        _proj_conv_kernel,
        grid=(nj, m // tm),
        in_specs=[pl.BlockSpec((tm, D_MODEL), lambda j, i: (i, 0)),
                  wspec(0), wspec(1), wspec(2), wspec(3),
                  pl.BlockSpec((CONV_WIDTH, PROJ_TN), lambda j, i: (0, j)),
                  pl.BlockSpec((1, PROJ_TN), lambda j, i: (0, j))],
        out_specs=pl.BlockSpec((tm, PROJ_TN), lambda j, i: (i, j)),
        out_shape=jax.ShapeDtypeStruct((m, D_CONV), BF16),
        name="proj_conv",
        scratch_shapes=[pltpu.VMEM((tm + 8, PROJ_TN), F32)],
        compiler_params=pltpu.CompilerParams(
            dimension_semantics=("arbitrary", "arbitrary"), vmem_limit_bytes=VMEM_LIMIT),
    )(h, w_in_l, w_in_l, w_in_l, w_in_l, conv_w_l, conv_gain_l.reshape(1, D_CONV))


def _strict_lower_neg(n):
    j = lax.broadcasted_iota(jnp.int32, (n, n), 0)
    s = lax.broadcasted_iota(jnp.int32, (n, n), 1)
    t = jnp.where(j > s, -1.0, 0.0).astype(BF16)
    return jnp.concatenate([t, t], axis=0)


def _sb_block(qt, kb, vb, run, tneg2, mask):
    s = lax.dot_general(qt, kb, (((1,), (1,)), ((), ())), preferred_element_type=F32)
    z = s * (1.0 / math.sqrt(HEAD_DIM))
    sp = jnp.maximum(z, 0.0) + jnp.log1p(jnp.exp(-jnp.abs(z)))
    log_beta = z - sp
    if mask is not None:
        sp = jnp.where(mask, sp, 0.0)
    hi = sp.astype(BF16)
    lo = (sp - hi.astype(F32)).astype(BF16)
    after = jnp.dot(jnp.concatenate([hi, lo], axis=1), tneg2, preferred_element_type=F32)
    a = jnp.exp(log_beta + after + run)
    if mask is not None:
        a = jnp.where(mask, a, 0.0)
    pv = jnp.dot(a.astype(BF16), vb, preferred_element_type=F32)
    new_run = run + (after[:, 0:1] - sp[:, 0:1])
    return pv, new_run


def _attn_kernel(q_ref, k_ref, v_ref, g_ref, gain_ref, o_ref):
    tneg_kv = _strict_lower_neg(KV_BLOCK)
    tneg_meta = _strict_lower_neg(META_BLOCK)
    gain = gain_ref[...]

    def causal(n):
        t = lax.broadcasted_iota(jnp.int32, (n, n), 0)
        s = lax.broadcasted_iota(jnp.int32, (n, n), 1)
        return s < t

    def finish(acc, rows):
        o = acc * (lambda g: g * jax.nn.sigmoid(g))(g_ref[rows, :])
        ms = jnp.sum(o * o, axis=-1, keepdims=True) * (1.0 / HEAD_DIM)
        o_ref[rows, :] = (o * lax.rsqrt(ms + EPS) * gain).astype(o_ref.dtype)

    meta_rows = pl.ds(0, META_BLOCK)
    acc, _ = _sb_block(q_ref[meta_rows, :], k_ref[meta_rows, :], v_ref[meta_rows, :],
                       jnp.zeros((META_BLOCK, 1), F32), tneg_meta, causal(META_BLOCK))
    finish(acc, meta_rows)

    diag_mask = causal(KV_BLOCK)

    def q_tile(i, carry):
        rows = pl.ds(pl.multiple_of(META_BLOCK + i * KV_BLOCK, 128), KV_BLOCK)
        qt = q_ref[rows, :]
        acc, run = _sb_block(qt, k_ref[rows, :], v_ref[rows, :],
                             jnp.zeros((KV_BLOCK, 1), F32), tneg_kv, diag_mask)

        def kv_step(jj, c):
            acc, run = c
            cols = pl.ds(pl.multiple_of(META_BLOCK + (i - 1 - jj) * KV_BLOCK, 128), KV_BLOCK)
            pv, run = _sb_block(qt, k_ref[cols, :], v_ref[cols, :], run, tneg_kv, None)
            return acc + pv, run

        acc, run = lax.fori_loop(0, i, kv_step, (acc, run))
        pv, _ = _sb_block(qt, k_ref[meta_rows, :], v_ref[meta_rows, :], run, tneg_meta, None)
        finish(acc + pv, rows)
        return carry

    lax.fori_loop(0, N_KV_BLOCKS, q_tile, 0)


def _attention(q, k, v, g, gain_l, batch):
    m = q.shape[0]
    spec = pl.BlockSpec((L_PAD, HEAD_DIM), lambda b, h: (b, h))
    return pl.pallas_call(
        _attn_kernel,
        grid=(batch, ATTN_HEADS),
        in_specs=[spec, spec, spec, spec,
                  pl.BlockSpec((1, HEAD_DIM), lambda b, h: (0, h))],
        out_specs=spec,
        out_shape=jax.ShapeDtypeStruct((m, D_ATTN), BF16),
        name="sb_attention",
        compiler_params=pltpu.CompilerParams(
            dimension_semantics=("arbitrary", "arbitrary"), vmem_limit_bytes=VMEM_LIMIT),
    )(q, k, v, g, gain_l.reshape(1, D_ATTN))


def _out_proj_kernel(o_ref, y_ref, wo_ref, wy_ref, x_ref, out_ref):
    acc = jnp.dot(o_ref[...], wo_ref[...], preferred_element_type=F32)
    acc = acc + jnp.dot(y_ref[...], wy_ref[...], preferred_element_type=F32)
    out_ref[...] = x_ref[...] + acc


def _out_proj(o, y, w_out_l, x2d, tm):
    m = o.shape[0]
    nj = D_MODEL // OUT_TN
    return pl.pallas_call(
        _out_proj_kernel,
        grid=(m // tm, nj),
        in_specs=[pl.BlockSpec((tm, D_ATTN), lambda i, j: (i, 0)),
                  pl.BlockSpec((tm, D_CONV), lambda i, j: (i, 0)),
                  pl.BlockSpec((D_ATTN, OUT_TN), lambda i, j: (0, j)),
                  pl.BlockSpec((D_CONV, OUT_TN), lambda i, j: (1, j)),
                  pl.BlockSpec((tm, OUT_TN), lambda i, j: (i, j))],
        out_specs=pl.BlockSpec((tm, OUT_TN), lambda i, j: (i, j)),
        out_shape=jax.ShapeDtypeStruct((m, D_MODEL), F32),
        name="out_proj",
        compiler_params=pltpu.CompilerParams(
            dimension_semantics=("arbitrary", "arbitrary"), vmem_limit_bytes=VMEM_LIMIT),
    )(o, y, w_out_l, w_out_l, x2d)


def kernel(x, meta_tokens, norm_g, w_in, conv_w, attn_norm_g, conv_norm_g, w_out, final_norm_g):
    batch = x.shape[0]
    depth = w_in.shape[0]
    zeros = jnp.zeros((batch, META_BLOCK - N_META, D_MODEL), x.dtype)
    meta = jnp.broadcast_to(meta_tokens.astype(x.dtype)[None], (batch, N_META, D_MODEL))
    hs = jnp.concatenate([zeros, meta, x], axis=1).reshape(batch * L_PAD, D_MODEL)
    m = hs.shape[0]
    tm = _row_tile(m)
    w_in_bf = w_in.astype(BF16)
    w_out_bf = w_out.astype(BF16)
    for l in range(depth):
        h = _rmsnorm(hs, norm_g[l], BF16, NORM_TM)
        q, k, v, g = _proj_attn(h, w_in_bf[l], tm)
        y = _proj_conv(h, w_in_bf[l], conv_w[l], conv_norm_g[l], tm)
        o = _attention(q, k, v, g, attn_norm_g[l], batch)
        hs = _out_proj(o, y, w_out_bf[l], hs, tm)
    return _final_norm(hs.reshape(batch, L_PAD, D_MODEL), final_norm_g)
```

```python
import functools
import math

import jax
import jax.numpy as jnp
from jax import lax
from jax.experimental import pallas as pl
from jax.experimental.pallas import tpu as pltpu

D_MODEL = 4096
SEQ = 4096
N_META = 16
HEAD_DIM = 128
ATTN_HEADS = 16
D_ATTN = ATTN_HEADS * HEAD_DIM
D_CONV = D_MODEL - D_ATTN
CONV_WIDTH = 3
EPS = 1e-6

LANES = 128
META_BLOCK = 128
L_PAD = META_BLOCK + SEQ
KV_BLOCK = 256
N_KV_BLOCKS = SEQ // KV_BLOCK
HEADS_PER_STEP = 2
KV_UNROLL = 2
PROJ_TN = 256
OUT_TN = 512
NORM_TM = 256
VMEM_LIMIT = 56 * 1024 * 1024

F32 = jnp.float32
BF16 = jnp.bfloat16


def _row_tile(m):
    for t in (1056, 768, 704, 528, 384, 256, 128):
        if m % t == 0:
            return t
    raise ValueError(f"unsupported row count {m}")


def _rmsnorm_kernel(x_ref, g_ref, o_ref):
    x = x_ref[...]
    ms = jnp.sum(x * x, axis=-1, keepdims=True) * (1.0 / D_MODEL)
    o_ref[...] = (x * lax.rsqrt(ms + EPS) * g_ref[...]).astype(o_ref.dtype)


def _rmsnorm(x2d, g, out_dtype, tm):
    m = x2d.shape[0]
    return pl.pallas_call(
        _rmsnorm_kernel,
        grid=(m // tm,),
        in_specs=[pl.BlockSpec((tm, D_MODEL), lambda i: (i, 0)),
                  pl.BlockSpec((1, D_MODEL), lambda i: (0, 0))],
        out_specs=pl.BlockSpec((tm, D_MODEL), lambda i: (i, 0)),
        out_shape=jax.ShapeDtypeStruct((m, D_MODEL), out_dtype),
        name="rmsnorm",
        compiler_params=pltpu.CompilerParams(
            dimension_semantics=("arbitrary",), vmem_limit_bytes=VMEM_LIMIT),
    )(x2d, g.reshape(1, D_MODEL))


def _final_norm_kernel(x_ref, g_ref, o_ref):
    x = x_ref[0]
    ms = jnp.sum(x * x, axis=-1, keepdims=True) * (1.0 / D_MODEL)
    o_ref[0] = x * lax.rsqrt(ms + EPS) * g_ref[...]


def _final_norm(hs3d, g):
    b = hs3d.shape[0]
    off = META_BLOCK // 128
    return pl.pallas_call(
        _final_norm_kernel,
        grid=(b, SEQ // 128),
        in_specs=[pl.BlockSpec((1, 128, D_MODEL), lambda bi, i: (bi, i + off, 0)),
                  pl.BlockSpec((1, D_MODEL), lambda bi, i: (0, 0))],
        out_specs=pl.BlockSpec((1, 128, D_MODEL), lambda bi, i: (bi, i, 0)),
        out_shape=jax.ShapeDtypeStruct((b, SEQ, D_MODEL), F32),
        name="final_norm",
        compiler_params=pltpu.CompilerParams(
            dimension_semantics=("arbitrary", "arbitrary"), vmem_limit_bytes=VMEM_LIMIT),
    )(hs3d, g.reshape(1, D_MODEL))


def _proj_attn_kernel(h_ref, wq_ref, wk_ref, wv_ref, wg_ref, q_ref, k_ref, v_ref, g_ref):
    h = h_ref[...]
    q_ref[...] = jnp.dot(h, wq_ref[...], preferred_element_type=F32).astype(BF16)
    k_ref[...] = jnp.dot(h, wk_ref[...], preferred_element_type=F32).astype(BF16)
    v_ref[...] = jnp.dot(h, wv_ref[...], preferred_element_type=F32).astype(BF16)
    g_ref[...] = jnp.dot(h, wg_ref[...], preferred_element_type=F32)


def _proj_attn(h, w_in, layer, tm):
    m = h.shape[0]
    nj = D_ATTN // PROJ_TN

    def wspec(seg):
        return pl.BlockSpec((None, D_MODEL, PROJ_TN),
                            lambda j, i, seg=seg: (layer, 0, seg * nj + j))

    ospec = pl.BlockSpec((tm, PROJ_TN), lambda j, i: (i, j))
    return pl.pallas_call(
        _proj_attn_kernel,
        grid=(nj, m // tm),
        in_specs=[pl.BlockSpec((tm, D_MODEL), lambda j, i: (i, 0)),
                  wspec(0), wspec(1), wspec(2), wspec(3)],
        out_specs=[ospec, ospec, ospec, ospec],
        out_shape=[jax.ShapeDtypeStruct((m, D_ATTN), BF16)] * 3
        + [jax.ShapeDtypeStruct((m, D_ATTN), F32)],
        name="proj_attn",
        compiler_params=pltpu.CompilerParams(
            dimension_semantics=("arbitrary", "arbitrary"), vmem_limit_bytes=VMEM_LIMIT),
    )(h, w_in, w_in, w_in, w_in)


def _proj_conv_kernel(h_ref, wb_ref, wc_ref, wh_ref, wz_ref, cw_ref, gain_ref, y_ref,
                      u_ref):
    i = pl.program_id(1)
    tm = h_ref.shape[0]
    h = h_ref[...]

    @pl.when(i == 0)
    def _():
        u_ref[0:8, :] = jnp.zeros((8, PROJ_TN), F32)

    @pl.when(i > 0)
    def _():
        u_ref[0:8, :] = u_ref[tm:tm + 8, :]

    u_ref[8:8 + tm, :] = (jnp.dot(h, wc_ref[...], preferred_element_type=F32)
                          * jnp.dot(h, wh_ref[...], preferred_element_type=F32))
    cw = cw_ref[...]
    conv = u_ref[6:6 + tm, :] * cw[0:1, :]
    conv = conv + u_ref[7:7 + tm, :] * cw[1:2, :]
    conv = conv + u_ref[8:8 + tm, :] * cw[2:3, :]
    y = jnp.dot(h, wb_ref[...], preferred_element_type=F32) * conv
    z = jnp.dot(h, wz_ref[...], preferred_element_type=F32)
    y = y * (z * jax.nn.sigmoid(z))
    gain = gain_ref[...]
    for c in range(PROJ_TN // LANES):
        sl = slice(c * LANES, (c + 1) * LANES)
        yc = y[:, sl]
        ms = jnp.sum(yc * yc, axis=-1, keepdims=True) * (1.0 / LANES)
        y_ref[:, sl] = (yc * lax.rsqrt(ms + EPS) * gain[:, sl]).astype(BF16)


def _proj_conv(h, w_in, layer, conv_w_l, conv_gain_l, tm):
    m = h.shape[0]
    nj = D_CONV // PROJ_TN
    base = 4 * D_ATTN // PROJ_TN

    def wspec(seg):
        return pl.BlockSpec((None, D_MODEL, PROJ_TN),
                            lambda j, i, seg=seg: (layer, 0, base + seg * nj + j))

    return pl.pallas_call(
        _proj_conv_kernel,
        grid=(nj, m // tm),
        in_specs=[pl.BlockSpec((tm, D_MODEL), lambda j, i: (i, 0)),
                  wspec(0), wspec(1), wspec(2), wspec(3),
                  pl.BlockSpec((CONV_WIDTH, PROJ_TN), lambda j, i: (0, j)),
                  pl.BlockSpec((1, PROJ_TN), lambda j, i: (0, j))],
        out_specs=pl.BlockSpec((tm, PROJ_TN), lambda j, i: (i, j)),
        out_shape=jax.ShapeDtypeStruct((m, D_CONV), BF16),
        name="proj_conv",
        scratch_shapes=[pltpu.VMEM((tm + 8, PROJ_TN), F32)],
        compiler_params=pltpu.CompilerParams(
            dimension_semantics=("arbitrary", "arbitrary"), vmem_limit_bytes=VMEM_LIMIT),
    )(h, w_in, w_in, w_in, w_in, conv_w_l, conv_gain_l.reshape(1, D_CONV))


def _strict_lower_neg(n):
    j = lax.broadcasted_iota(jnp.int32, (n, n), 0)
    s = lax.broadcasted_iota(jnp.int32, (n, n), 1)
    t = jnp.where(j > s, -1.0, 0.0).astype(BF16)
    return jnp.concatenate([t, t], axis=0)


def _causal(n):
    t = lax.broadcasted_iota(jnp.int32, (n, n), 0)
    s = lax.broadcasted_iota(jnp.int32, (n, n), 1)
    return s < t


def _sb_chains(q_tiles, k_ref, v_ref, heads, chains, carry):
    s = [lax.dot_general(q_tiles[h], k_ref[cols, heads[h]], (((1,), (1,)), ((), ())),
                         preferred_element_type=F32) for h, cols, _, _ in chains]
    log_beta, sp = [], []
    for si, (_, _, _, mask) in zip(s, chains):
        z = si * (1.0 / math.sqrt(HEAD_DIM))
        spi = jnp.maximum(z, 0.0) + jnp.log(1.0 + jnp.exp(-jnp.abs(z)))
        log_beta.append(z - spi)
        sp.append(spi if mask is None else jnp.where(mask, spi, 0.0))
    after = []
    for spi, (_, _, tneg2, _) in zip(sp, chains):
        hi = spi.astype(BF16)
        lo = (spi - hi.astype(F32)).astype(BF16)
        after.append(jnp.dot(jnp.concatenate([hi, lo], axis=1), tneg2,
                             preferred_element_type=F32))
    accs = [c[0] for c in carry]
    runs = [c[1] for c in carry]
    weights = []
    for lbi, spi, afi, (h, _, _, mask) in zip(log_beta, sp, after, chains):
        a = jnp.exp(lbi + afi + runs[h])
        if mask is not None:
            a = jnp.where(mask, a, 0.0)
        weights.append(a.astype(BF16))
        runs[h] = runs[h] + (afi[:, 0:1] - spi[:, 0:1])
    for a, (h, cols, _, _) in zip(weights, chains):
        pv = jnp.dot(a, v_ref[cols, heads[h]], preferred_element_type=F32)
        accs[h] = pv if accs[h] is None else accs[h] + pv
    return list(zip(accs, runs))


def _attn_kernel(q_ref, k_ref, v_ref, g_ref, gain_ref, o_ref):
    tneg_kv = _strict_lower_neg(KV_BLOCK)
    tneg_meta = _strict_lower_neg(META_BLOCK)
    heads = [slice(h * HEAD_DIM, (h + 1) * HEAD_DIM) for h in range(HEADS_PER_STEP)]
    meta_rows = pl.ds(0, META_BLOCK)

    def finish(accs, rows):
        for acc, hd in zip(accs, heads):
            g = g_ref[rows, hd]
            o = acc * (g * jax.nn.sigmoid(g))
            ms = jnp.sum(o * o, axis=-1, keepdims=True) * (1.0 / HEAD_DIM)
            o_ref[rows, hd] = (o * lax.rsqrt(ms + EPS) * gain_ref[:, hd]).astype(o_ref.dtype)

    def blocks(qts, cols_list, carry, tneg2, mask):
        chains = [(h, cols, tneg2, mask) for cols in cols_list
                  for h in range(HEADS_PER_STEP)]
        return _sb_chains(qts, k_ref, v_ref, heads, chains, carry)

    qts = [q_ref[meta_rows, hd] for hd in heads]
    carry = blocks(qts, [meta_rows], [(None, jnp.zeros((META_BLOCK, 1), F32))] * HEADS_PER_STEP,
                   tneg_meta, _causal(META_BLOCK))
    finish([c[0] for c in carry], meta_rows)

    diag_mask = _causal(KV_BLOCK)

    def kv_rows(j):
        return pl.ds(pl.multiple_of(META_BLOCK + j * KV_BLOCK, 128), KV_BLOCK)

    def q_tile(i, _):
        rows = kv_rows(i)
        qts = [q_ref[rows, hd] for hd in heads]
        carry = blocks(qts, [rows], [(None, jnp.zeros((KV_BLOCK, 1), F32))] * HEADS_PER_STEP,
                       tneg_kv, diag_mask)

        def kv_group(jj, c):
            first = i - 1 - KV_UNROLL * jj
            return blocks(qts, [kv_rows(first - u) for u in range(KV_UNROLL)], c, tneg_kv, None)

        carry = lax.fori_loop(0, i // KV_UNROLL, kv_group, carry)
        carry = lax.fori_loop(
            0, i % KV_UNROLL,
            lambda jj, c: blocks(qts, [kv_rows(i % KV_UNROLL - 1 - jj)], c, tneg_kv, None), carry)
        carry = blocks(qts, [meta_rows], carry, tneg_meta, None)
        finish([c[0] for c in carry], rows)
        return 0

    lax.fori_loop(0, N_KV_BLOCKS, q_tile, 0)


def _attention(q, k, v, g, gain_l, batch):
    m = q.shape[0]
    width = HEADS_PER_STEP * HEAD_DIM
    spec = pl.BlockSpec((L_PAD, width), lambda b, h: (b, h))
    return pl.pallas_call(
        _attn_kernel,
        grid=(batch, ATTN_HEADS // HEADS_PER_STEP),
        in_specs=[spec, spec, spec, spec,
                  pl.BlockSpec((1, width), lambda b, h: (0, h))],
        out_specs=spec,
        out_shape=jax.ShapeDtypeStruct((m, D_ATTN), BF16),
        name="sb_attention",
        compiler_params=pltpu.CompilerParams(
            dimension_semantics=("arbitrary", "arbitrary"), vmem_limit_bytes=VMEM_LIMIT),
    )(q, k, v, g, gain_l.reshape(1, D_ATTN))


def _out_proj_kernel(o_ref, y_ref, wo_ref, wy_ref, x_ref, out_ref):
    acc = jnp.dot(o_ref[...], wo_ref[...], preferred_element_type=F32)
    acc = acc + jnp.dot(y_ref[...], wy_ref[...], preferred_element_type=F32)
    out_ref[...] = x_ref[...] + acc


def _out_proj(o, y, w_out, layer, x2d, tm):
    m = o.shape[0]
    nj = D_MODEL // OUT_TN
    return pl.pallas_call(
        _out_proj_kernel,
        grid=(m // tm, nj),
        in_specs=[pl.BlockSpec((tm, D_ATTN), lambda i, j: (i, 0)),
                  pl.BlockSpec((tm, D_CONV), lambda i, j: (i, 0)),
                  pl.BlockSpec((None, D_ATTN, OUT_TN), lambda i, j: (layer, 0, j)),
                  pl.BlockSpec((None, D_CONV, OUT_TN), lambda i, j: (layer, 1, j)),
                  pl.BlockSpec((tm, OUT_TN), lambda i, j: (i, j))],
        out_specs=pl.BlockSpec((tm, OUT_TN), lambda i, j: (i, j)),
        out_shape=jax.ShapeDtypeStruct((m, D_MODEL), F32),
        name="out_proj",
        compiler_params=pltpu.CompilerParams(
            dimension_semantics=("arbitrary", "arbitrary"), vmem_limit_bytes=VMEM_LIMIT),
    )(o, y, w_out, w_out, x2d)


def kernel(x, meta_tokens, norm_g, w_in, conv_w, attn_norm_g, conv_norm_g, w_out, final_norm_g):
    batch = x.shape[0]
    depth = w_in.shape[0]
    zeros = jnp.zeros((batch, META_BLOCK - N_META, D_MODEL), x.dtype)
    meta = jnp.broadcast_to(meta_tokens.astype(x.dtype)[None], (batch, N_META, D_MODEL))
    hs = jnp.concatenate([zeros, meta, x], axis=1).reshape(batch * L_PAD, D_MODEL)
    m = hs.shape[0]
    tm = _row_tile(m)
    w_in_bf = w_in.astype(BF16)
    w_out_bf = w_out.astype(BF16)
    for l in range(depth):
        h = _rmsnorm(hs, norm_g[l], BF16, NORM_TM)
        q, k, v, g = _proj_attn(h, w_in_bf, l, tm)
        y = _proj_conv(h, w_in_bf, l, conv_w[l], conv_norm_g[l], tm)
        o = _attention(q, k, v, g, attn_norm_g[l], batch)
        hs = _out_proj(o, y, w_out_bf, l, hs, tm)
    return _final_norm(hs.reshape(batch, L_PAD, D_MODEL), final_norm_g)
```

```python
import functools
import math

import jax
import jax.numpy as jnp
from jax import lax
from jax.experimental import pallas as pl
from jax.experimental.pallas import tpu as pltpu

D_MODEL = 4096
SEQ = 4096
N_META = 16
HEAD_DIM = 128
ATTN_HEADS = 16
D_ATTN = ATTN_HEADS * HEAD_DIM
D_CONV = D_MODEL - D_ATTN
CONV_WIDTH = 3
EPS = 1e-6

LANES = 128
META_BLOCK = 128
L_PAD = META_BLOCK + SEQ
KV_BLOCK = 256
Q_SUPER = 2 * KV_BLOCK
HEADS_PER_STEP = 2
MASKED_LOG_WEIGHT = -1e30
PROJ_TN = 256
OUT_TN = 512
NORM_TM = 256
VMEM_LIMIT = 56 * 1024 * 1024

F32 = jnp.float32
BF16 = jnp.bfloat16


def _row_tile(m):
    for t in (1056, 768, 704, 528, 384, 256, 128):
        if m % t == 0:
            return t
    raise ValueError(f"unsupported row count {m}")


def _rmsnorm_kernel(x_ref, g_ref, o_ref):
    x = x_ref[...]
    ms = jnp.sum(x * x, axis=-1, keepdims=True) * (1.0 / D_MODEL)
    o_ref[...] = (x * lax.rsqrt(ms + EPS) * g_ref[...]).astype(o_ref.dtype)


def _rmsnorm(x2d, g, out_dtype, tm):
    m = x2d.shape[0]
    return pl.pallas_call(
        _rmsnorm_kernel,
        grid=(m // tm,),
        in_specs=[pl.BlockSpec((tm, D_MODEL), lambda i: (i, 0)),
                  pl.BlockSpec((1, D_MODEL), lambda i: (0, 0))],
        out_specs=pl.BlockSpec((tm, D_MODEL), lambda i: (i, 0)),
        out_shape=jax.ShapeDtypeStruct((m, D_MODEL), out_dtype),
        name="rmsnorm",
        compiler_params=pltpu.CompilerParams(
            dimension_semantics=("arbitrary",), vmem_limit_bytes=VMEM_LIMIT),
    )(x2d, g.reshape(1, D_MODEL))


def _final_norm_kernel(x_ref, g_ref, o_ref):
    x = x_ref[0]
    ms = jnp.sum(x * x, axis=-1, keepdims=True) * (1.0 / D_MODEL)
    o_ref[0] = x * lax.rsqrt(ms + EPS) * g_ref[...]


def _final_norm(hs3d, g):
    b = hs3d.shape[0]
    off = META_BLOCK // 128
    return pl.pallas_call(
        _final_norm_kernel,
        grid=(b, SEQ // 128),
        in_specs=[pl.BlockSpec((1, 128, D_MODEL), lambda bi, i: (bi, i + off, 0)),
                  pl.BlockSpec((1, D_MODEL), lambda bi, i: (0, 0))],
        out_specs=pl.BlockSpec((1, 128, D_MODEL), lambda bi, i: (bi, i, 0)),
        out_shape=jax.ShapeDtypeStruct((b, SEQ, D_MODEL), F32),
        name="final_norm",
        compiler_params=pltpu.CompilerParams(
            dimension_semantics=("arbitrary", "arbitrary"), vmem_limit_bytes=VMEM_LIMIT),
    )(hs3d, g.reshape(1, D_MODEL))


def _proj_attn_kernel(h_ref, wq_ref, wk_ref, wv_ref, wg_ref, q_ref, k_ref, v_ref, g_ref):
    h = h_ref[...]
    q_ref[...] = jnp.dot(h, wq_ref[...], preferred_element_type=F32).astype(BF16)
    k_ref[...] = jnp.dot(h, wk_ref[...], preferred_element_type=F32).astype(BF16)
    v_ref[...] = jnp.dot(h, wv_ref[...], preferred_element_type=F32).astype(BF16)
    g_ref[...] = jnp.dot(h, wg_ref[...], preferred_element_type=F32)


def _proj_attn(h, w_in, layer, tm):
    m = h.shape[0]
    nj = D_ATTN // PROJ_TN

    def wspec(seg):
        return pl.BlockSpec((None, D_MODEL, PROJ_TN),
                            lambda j, i, seg=seg: (layer, 0, seg * nj + j))

    ospec = pl.BlockSpec((tm, PROJ_TN), lambda j, i: (i, j))
    return pl.pallas_call(
        _proj_attn_kernel,
        grid=(nj, m // tm),
        in_specs=[pl.BlockSpec((tm, D_MODEL), lambda j, i: (i, 0)),
                  wspec(0), wspec(1), wspec(2), wspec(3)],
        out_specs=[ospec, ospec, ospec, ospec],
        out_shape=[jax.ShapeDtypeStruct((m, D_ATTN), BF16)] * 3
        + [jax.ShapeDtypeStruct((m, D_ATTN), F32)],
        name="proj_attn",
        compiler_params=pltpu.CompilerParams(
            dimension_semantics=("arbitrary", "arbitrary"), vmem_limit_bytes=VMEM_LIMIT),
    )(h, w_in, w_in, w_in, w_in)


def _proj_conv_kernel(h_ref, wb_ref, wc_ref, wh_ref, wz_ref, cw_ref, gain_ref, y_ref,
                      u_ref):
    i = pl.program_id(1)
    tm = h_ref.shape[0]
    h = h_ref[...]

    @pl.when(i == 0)
    def _():
        u_ref[0:8, :] = jnp.zeros((8, PROJ_TN), F32)

    @pl.when(i > 0)
    def _():
        u_ref[0:8, :] = u_ref[tm:tm + 8, :]

    u_ref[8:8 + tm, :] = (jnp.dot(h, wc_ref[...], preferred_element_type=F32)
                          * jnp.dot(h, wh_ref[...], preferred_element_type=F32))
    cw = cw_ref[...]
    conv = u_ref[6:6 + tm, :] * cw[0:1, :]
    conv = conv + u_ref[7:7 + tm, :] * cw[1:2, :]
    conv = conv + u_ref[8:8 + tm, :] * cw[2:3, :]
    y = jnp.dot(h, wb_ref[...], preferred_element_type=F32) * conv
    z = jnp.dot(h, wz_ref[...], preferred_element_type=F32)
    y = y * (z * jax.nn.sigmoid(z))
    gain = gain_ref[...]
    for c in range(PROJ_TN // LANES):
        sl = slice(c * LANES, (c + 1) * LANES)
        yc = y[:, sl]
        ms = jnp.sum(yc * yc, axis=-1, keepdims=True) * (1.0 / LANES)
        y_ref[:, sl] = (yc * lax.rsqrt(ms + EPS) * gain[:, sl]).astype(BF16)


def _proj_conv(h, w_in, layer, conv_w_l, conv_gain_l, tm):
    m = h.shape[0]
    nj = D_CONV // PROJ_TN
    base = 4 * D_ATTN // PROJ_TN

    def wspec(seg):
        return pl.BlockSpec((None, D_MODEL, PROJ_TN),
                            lambda j, i, seg=seg: (layer, 0, base + seg * nj + j))

    return pl.pallas_call(
        _proj_conv_kernel,
        grid=(nj, m // tm),
        in_specs=[pl.BlockSpec((tm, D_MODEL), lambda j, i: (i, 0)),
                  wspec(0), wspec(1), wspec(2), wspec(3),
                  pl.BlockSpec((CONV_WIDTH, PROJ_TN), lambda j, i: (0, j)),
                  pl.BlockSpec((1, PROJ_TN), lambda j, i: (0, j))],
        out_specs=pl.BlockSpec((tm, PROJ_TN), lambda j, i: (i, j)),
        out_shape=jax.ShapeDtypeStruct((m, D_CONV), BF16),
        name="proj_conv",
        scratch_shapes=[pltpu.VMEM((tm + 8, PROJ_TN), F32)],
        compiler_params=pltpu.CompilerParams(
            dimension_semantics=("arbitrary", "arbitrary"), vmem_limit_bytes=VMEM_LIMIT),
    )(h, w_in, w_in, w_in, w_in, conv_w_l, conv_gain_l.reshape(1, D_CONV))


def _strict_lower_neg(n):
    j = lax.broadcasted_iota(jnp.int32, (n, n), 0)
    s = lax.broadcasted_iota(jnp.int32, (n, n), 1)
    return jnp.where(j > s, -1.0, 0.0).astype(BF16)


def _causal(n):
    t = lax.broadcasted_iota(jnp.int32, (n, n), 0)
    s = lax.broadcasted_iota(jnp.int32, (n, n), 1)
    return s < t


def _scores(qt, kb):
    return lax.dot_general(qt, kb, (((1,), (1,)), ((), ())), preferred_element_type=F32)


def _softplus_parts(s, mask):
    z = s * (1.0 / math.sqrt(HEAD_DIM))
    sp = jnp.maximum(z, 0.0) + jnp.log(1.0 + jnp.exp(-jnp.abs(z)))
    return z - sp, (sp if mask is None else jnp.where(mask, sp, 0.0))


def _after(sp, tneg):
    return jnp.dot(sp.astype(BF16), tneg, preferred_element_type=F32)


def _pending(log_beta, sp, after, mask):
    e = log_beta + after
    if mask is not None:
        e = jnp.where(mask, e, MASKED_LOG_WEIGHT)
    return e, after[:, 0:1] - sp[:, 0:1]


def _front(qts, k_ref, heads, rows, tneg, mask):
    s = [_scores(qt, k_ref[rows, hd]) for qt, hd in zip(qts, heads)]
    parts = [_softplus_parts(si, mask) for si in s]
    after = [_after(sp, tneg) for _, sp in parts]
    return [_pending(lb, sp, af, mask) for (lb, sp), af in zip(parts, after)]


def _attn_kernel(q_ref, k_ref, v_ref, g_ref, gain_ref, o_ref, s_ref, acc_ref, run_ref, em_ref):
    tneg_kv = _strict_lower_neg(KV_BLOCK)
    tneg_meta = _strict_lower_neg(META_BLOCK)
    heads = [slice(h * HEAD_DIM, (h + 1) * HEAD_DIM) for h in range(HEADS_PER_STEP)]
    meta_rows = pl.ds(0, META_BLOCK)

    def finish(accs, rows):
        for acc, hd in zip(accs, heads):
            g = g_ref[rows, hd]
            o = acc * (g * jax.nn.sigmoid(g))
            ms = jnp.sum(o * o, axis=-1, keepdims=True) * (1.0 / HEAD_DIM)
            o_ref[rows, hd] = (o * lax.rsqrt(ms + EPS) * gain_ref[:, hd]).astype(o_ref.dtype)

    def weighted_values(ws, rows):
        return [jnp.dot(w, v_ref[rows, hd], preferred_element_type=F32)
                for w, hd in zip(ws, heads)]

    qts = [q_ref[meta_rows, hd] for hd in heads]
    pend = _front(qts, k_ref, heads, meta_rows, tneg_meta, _causal(META_BLOCK))
    finish(weighted_values([jnp.exp(e).astype(BF16) for e, _ in pend], meta_rows), meta_rows)

    diag_mask = _causal(KV_BLOCK)
    edge_mask = (lax.broadcasted_iota(jnp.int32, (Q_SUPER, KV_BLOCK), 1)
                 < lax.broadcasted_iota(jnp.int32, (Q_SUPER, KV_BLOCK), 0))
    hp = range(HEADS_PER_STEP)

    def kv_rows(j):
        return pl.ds(pl.multiple_of(META_BLOCK + j * KV_BLOCK, 128), KV_BLOCK)

    def super_tile(t, _):
        r0 = pl.multiple_of(META_BLOCK + t * Q_SUPER, 128)
        rows = pl.ds(r0, Q_SUPER)
        rows_lo = kv_rows(2 * t)
        rows_hi = kv_rows(2 * t + 1)
        q_all = [q_ref[rows, hd] for hd in heads]
        q_hi = [q_ref[rows_hi, hd] for hd in heads]

        s_hi = [_scores(q_hi[h], k_ref[rows_hi, heads[h]]) for h in hp]
        s_lo = [_scores(q_all[h], k_ref[rows_lo, heads[h]]) for h in hp]
        s_me = [_scores(q_all[h], k_ref[meta_rows, heads[h]]) for h in hp]
        first = kv_rows(jnp.maximum(2 * t - 1, 0))
        for h in hp:
            s_ref[h] = _scores(q_all[h], k_ref[first, heads[h]])
        parts_hi = [_softplus_parts(s, diag_mask) for s in s_hi]
        parts_lo = [_softplus_parts(s, edge_mask) for s in s_lo]
        parts_me = [_softplus_parts(s, None) for s in s_me]
        after_hi = [_after(sp, tneg_kv) for _, sp in parts_hi]
        after_lo = [_after(sp, tneg_kv) for _, sp in parts_lo]
        after_me = [_after(sp, tneg_meta) for _, sp in parts_me]
        for h in hp:
            e_hi, tot_hi = _pending(*parts_hi[h], after_hi[h], diag_mask)
            e_lo, tot_lo = _pending(*parts_lo[h], after_lo[h], edge_mask)
            em_ref[h] = _pending(*parts_me[h], after_me[h], None)[0]
            w_hi = jnp.exp(e_hi).astype(BF16)
            run = jnp.concatenate([jnp.zeros((KV_BLOCK, 1), F32), tot_hi], axis=0)
            w_lo = jnp.exp(e_lo + run).astype(BF16)
            run_ref[h] = run + tot_lo
            pv_hi = jnp.dot(w_hi, v_ref[rows_hi, heads[h]], preferred_element_type=F32)
            pv_lo = jnp.dot(w_lo, v_ref[rows_lo, heads[h]], preferred_element_type=F32)
            acc_ref[h, 0:KV_BLOCK] = pv_lo[0:KV_BLOCK]
            acc_ref[h, KV_BLOCK:Q_SUPER] = pv_lo[KV_BLOCK:Q_SUPER] + pv_hi

        def add_block(parts, after, j, runs):
            out = []
            for h in hp:
                e, tot = _pending(*parts[h], after[h], None)
                w = jnp.exp(e + runs[h]).astype(BF16)
                out.append(runs[h] + tot)
                acc_ref[h] += jnp.dot(w, v_ref[kv_rows(j), heads[h]],
                                      preferred_element_type=F32)
            return out

        def two_blocks(n, _):
            j = 2 * t - 1 - 2 * n
            parts_0 = [_softplus_parts(s_ref[h], None) for h in hp]
            s_1 = [_scores(q_ref[rows, heads[h]], k_ref[kv_rows(j - 1), heads[h]]) for h in hp]
            after_0 = [_after(sp, tneg_kv) for _, sp in parts_0]
            parts_1 = [_softplus_parts(s, None) for s in s_1]
            runs = add_block(parts_0, after_0, j, [run_ref[h] for h in hp])
            nxt = kv_rows(jnp.maximum(j - 2, 0))
            for h in hp:
                s_ref[h] = _scores(q_ref[rows, heads[h]], k_ref[nxt, heads[h]])
            after_1 = [_after(sp, tneg_kv) for _, sp in parts_1]
            runs = add_block(parts_1, after_1, j - 1, runs)
            for h in hp:
                run_ref[h] = runs[h]
            return 0

        lax.fori_loop(0, t, two_blocks, 0)
        accs = []
        for h in hp:
            w = jnp.exp(em_ref[h] + run_ref[h]).astype(BF16)
            accs.append(acc_ref[h] + jnp.dot(w, v_ref[meta_rows, heads[h]],
                                             preferred_element_type=F32))
        finish(accs, rows)
        return 0

    lax.fori_loop(0, SEQ // Q_SUPER, super_tile, 0)


def _attention(q, k, v, g, gain_l, batch):
    m = q.shape[0]
    width = HEADS_PER_STEP * HEAD_DIM
    spec = pl.BlockSpec((L_PAD, width), lambda b, h: (b, h))
    return pl.pallas_call(
        _attn_kernel,
        grid=(batch, ATTN_HEADS // HEADS_PER_STEP),
        in_specs=[spec, spec, spec, spec,
                  pl.BlockSpec((1, width), lambda b, h: (0, h))],
        out_specs=spec,
        out_shape=jax.ShapeDtypeStruct((m, D_ATTN), BF16),
        scratch_shapes=[pltpu.VMEM((HEADS_PER_STEP, Q_SUPER, KV_BLOCK), F32),
                        pltpu.VMEM((HEADS_PER_STEP, Q_SUPER, HEAD_DIM), F32),
                        pltpu.VMEM((HEADS_PER_STEP, Q_SUPER, 1), F32),
                        pltpu.VMEM((HEADS_PER_STEP, Q_SUPER, META_BLOCK), F32)],
        name="sb_attention",
        compiler_params=pltpu.CompilerParams(
            dimension_semantics=("arbitrary", "arbitrary"), vmem_limit_bytes=VMEM_LIMIT),
    )(q, k, v, g, gain_l.reshape(1, D_ATTN))


def _out_proj_kernel(o_ref, y_ref, wo_ref, wy_ref, x_ref, out_ref):
    acc = jnp.dot(o_ref[...], wo_ref[...], preferred_element_type=F32)
    acc = acc + jnp.dot(y_ref[...], wy_ref[...], preferred_element_type=F32)
    out_ref[...] = x_ref[...] + acc


def _out_proj(o, y, w_out, layer, x2d, tm):
    m = o.shape[0]
    nj = D_MODEL // OUT_TN
    return pl.pallas_call(
        _out_proj_kernel,
        grid=(m // tm, nj),
        in_specs=[pl.BlockSpec((tm, D_ATTN), lambda i, j: (i, 0)),
                  pl.BlockSpec((tm, D_CONV), lambda i, j: (i, 0)),
                  pl.BlockSpec((None, D_ATTN, OUT_TN), lambda i, j: (layer, 0, j)),
                  pl.BlockSpec((None, D_CONV, OUT_TN), lambda i, j: (layer, 1, j)),
                  pl.BlockSpec((tm, OUT_TN), lambda i, j: (i, j))],
        out_specs=pl.BlockSpec((tm, OUT_TN), lambda i, j: (i, j)),
        out_shape=jax.ShapeDtypeStruct((m, D_MODEL), F32),
        name="out_proj",
        compiler_params=pltpu.CompilerParams(
            dimension_semantics=("arbitrary", "arbitrary"), vmem_limit_bytes=VMEM_LIMIT),
    )(o, y, w_out, w_out, x2d)


def kernel(x, meta_tokens, norm_g, w_in, conv_w, attn_norm_g, conv_norm_g, w_out, final_norm_g):
    batch = x.shape[0]
    depth = w_in.shape[0]
    zeros = jnp.zeros((batch, META_BLOCK - N_META, D_MODEL), x.dtype)
    meta = jnp.broadcast_to(meta_tokens.astype(x.dtype)[None], (batch, N_META, D_MODEL))
    hs = jnp.concatenate([zeros, meta, x], axis=1).reshape(batch * L_PAD, D_MODEL)
    m = hs.shape[0]
    tm = _row_tile(m)
    w_in_bf = w_in.astype(BF16)
    w_out_bf = w_out.astype(BF16)
    for l in range(depth):
        h = _rmsnorm(hs, norm_g[l], BF16, NORM_TM)
        q, k, v, g = _proj_attn(h, w_in_bf, l, tm)
        y = _proj_conv(h, w_in_bf, l, conv_w[l], conv_norm_g[l], tm)
        o = _attention(q, k, v, g, attn_norm_g[l], batch)
        hs = _out_proj(o, y, w_out_bf, l, hs, tm)
    return _final_norm(hs.reshape(batch, L_PAD, D_MODEL), final_norm_g)
```

```python
import math

import jax
import jax.numpy as jnp
from jax import lax
from jax.experimental import pallas as pl
from jax.experimental.pallas import tpu as pltpu

D_MODEL = 4096
SEQ = 4096
N_META = 16
HEAD_DIM = 128
ATTN_HEADS = 16
D_ATTN = ATTN_HEADS * HEAD_DIM
D_CONV = D_MODEL - D_ATTN
CONV_WIDTH = 3
EPS = 1e-6

LANES = 128
META_BLOCK = 128
L_PAD = META_BLOCK + SEQ
KV_BLOCK = 256
Q_SUPER = 2 * KV_BLOCK
HEADS_PER_STEP = 2
MASKED_LOG_WEIGHT = -1e30
PROJ_TN = 256
PROJ_TM = 528
CONV_CHUNK = 176
OUT_TN = 512
OUT_TM = 1056
NORM_TM = 256
VMEM_LIMIT = 60 * 1024 * 1024

F32 = jnp.float32
BF16 = jnp.bfloat16


def _rmsnorm_kernel(x_ref, g_ref, o_ref):
    x = x_ref[...]
    ms = jnp.sum(x * x, axis=-1, keepdims=True) * (1.0 / D_MODEL)
    o_ref[...] = (x * lax.rsqrt(ms + EPS) * g_ref[...]).astype(o_ref.dtype)


def _rmsnorm(x2d, g, out_dtype, tm):
    m = x2d.shape[0]
    return pl.pallas_call(
        _rmsnorm_kernel,
        grid=(m // tm,),
        in_specs=[pl.BlockSpec((tm, D_MODEL), lambda i: (i, 0)),
                  pl.BlockSpec((1, D_MODEL), lambda i: (0, 0))],
        out_specs=pl.BlockSpec((tm, D_MODEL), lambda i: (i, 0)),
        out_shape=jax.ShapeDtypeStruct((m, D_MODEL), out_dtype),
        name="rmsnorm",
        compiler_params=pltpu.CompilerParams(
            dimension_semantics=("arbitrary",), vmem_limit_bytes=VMEM_LIMIT),
    )(x2d, g.reshape(1, D_MODEL))


def _final_norm_kernel(x_ref, g_ref, o_ref):
    x = x_ref[0]
    ms = jnp.sum(x * x, axis=-1, keepdims=True) * (1.0 / D_MODEL)
    o_ref[0] = x * lax.rsqrt(ms + EPS) * g_ref[...]


def _final_norm(hs3d, g):
    b = hs3d.shape[0]
    off = META_BLOCK // 128
    return pl.pallas_call(
        _final_norm_kernel,
        grid=(b, SEQ // 128),
        in_specs=[pl.BlockSpec((1, 128, D_MODEL), lambda bi, i: (bi, i + off, 0)),
                  pl.BlockSpec((1, D_MODEL), lambda bi, i: (0, 0))],
        out_specs=pl.BlockSpec((1, 128, D_MODEL), lambda bi, i: (bi, i, 0)),
        out_shape=jax.ShapeDtypeStruct((b, SEQ, D_MODEL), F32),
        name="final_norm",
        compiler_params=pltpu.CompilerParams(
            dimension_semantics=("arbitrary", "arbitrary"), vmem_limit_bytes=VMEM_LIMIT),
    )(hs3d, g.reshape(1, D_MODEL))


def _cast_weights_once(w_refs, wbf_ref):
    @pl.when(pl.program_id(1) == 0)
    def _():
        for seg, w_ref in enumerate(w_refs):
            wbf_ref[seg] = w_ref[...].astype(BF16)


def _proj_attn_kernel(h_ref, wq_ref, wk_ref, wv_ref, wg_ref, q_ref, k_ref, v_ref, g_ref,
                      wbf_ref):
    _cast_weights_once((wq_ref, wk_ref, wv_ref, wg_ref), wbf_ref)
    h = h_ref[...]
    q = jnp.dot(h, wbf_ref[0], preferred_element_type=F32)
    q_ref[...] = (q * (1.0 / math.sqrt(HEAD_DIM))).astype(BF16)
    k_ref[...] = jnp.dot(h, wbf_ref[1], preferred_element_type=F32).astype(BF16)
    v_ref[...] = jnp.dot(h, wbf_ref[2], preferred_element_type=F32).astype(BF16)
    g_ref[...] = jnp.dot(h, wbf_ref[3], preferred_element_type=F32)


def _proj_attn(h, w_in, layer, tm):
    m = h.shape[0]
    nj = D_ATTN // PROJ_TN

    def wspec(seg):
        return pl.BlockSpec((None, D_MODEL, PROJ_TN),
                            lambda j, i, seg=seg: (layer, 0, seg * nj + j))

    ospec = pl.BlockSpec((tm, PROJ_TN), lambda j, i: (i, j))
    return pl.pallas_call(
        _proj_attn_kernel,
        grid=(nj, m // tm),
        in_specs=[pl.BlockSpec((tm, D_MODEL), lambda j, i: (i, 0)),
                  wspec(0), wspec(1), wspec(2), wspec(3)],
        out_specs=[ospec, ospec, ospec, ospec],
        out_shape=[jax.ShapeDtypeStruct((m, D_ATTN), BF16)] * 3
        + [jax.ShapeDtypeStruct((m, D_ATTN), F32)],
        scratch_shapes=[pltpu.VMEM((4, D_MODEL, PROJ_TN), BF16)],
        name="proj_attn",
        compiler_params=pltpu.CompilerParams(
            dimension_semantics=("arbitrary", "arbitrary"), vmem_limit_bytes=VMEM_LIMIT),
    )(h, w_in, w_in, w_in, w_in)


def _proj_conv_kernel(h_ref, wb_ref, wc_ref, wh_ref, wz_ref, cw_ref, gain_ref, y_ref,
                      wbf_ref, u_ref):
    i = pl.program_id(1)
    tm = h_ref.shape[0]
    _cast_weights_once((wb_ref, wc_ref, wh_ref, wz_ref), wbf_ref)

    @pl.when(i == 0)
    def _():
        u_ref[0:8, :] = jnp.zeros((8, PROJ_TN), F32)

    @pl.when(i > 0)
    def _():
        u_ref[0:8, :] = u_ref[tm:tm + 8, :]

    cw = cw_ref[...]
    gain = gain_ref[...]
    for r0 in range(0, tm, CONV_CHUNK):
        h = h_ref[r0:r0 + CONV_CHUNK, :]
        u_ref[8 + r0:8 + r0 + CONV_CHUNK, :] = (
            jnp.dot(h, wbf_ref[1], preferred_element_type=F32)
            * jnp.dot(h, wbf_ref[2], preferred_element_type=F32))
        conv = u_ref[6 + r0:6 + r0 + CONV_CHUNK, :] * cw[0:1, :]
        conv = conv + u_ref[7 + r0:7 + r0 + CONV_CHUNK, :] * cw[1:2, :]
        conv = conv + u_ref[8 + r0:8 + r0 + CONV_CHUNK, :] * cw[2:3, :]
        y = jnp.dot(h, wbf_ref[0], preferred_element_type=F32) * conv
        z = jnp.dot(h, wbf_ref[3], preferred_element_type=F32)
        y = y * (z * jax.nn.sigmoid(z))
        for c in range(PROJ_TN // LANES):
            sl = slice(c * LANES, (c + 1) * LANES)
            yc = y[:, sl]
            ms = jnp.sum(yc * yc, axis=-1, keepdims=True) * (1.0 / LANES)
            y_ref[r0:r0 + CONV_CHUNK, sl] = (yc * lax.rsqrt(ms + EPS) * gain[:, sl]).astype(BF16)


def _proj_conv(h, w_in, layer, conv_w_l, conv_gain_l, tm):
    m = h.shape[0]
    nj = D_CONV // PROJ_TN
    base = 4 * D_ATTN // PROJ_TN

    def wspec(seg):
        return pl.BlockSpec((None, D_MODEL, PROJ_TN),
                            lambda j, i, seg=seg: (layer, 0, base + seg * nj + j))

    return pl.pallas_call(
        _proj_conv_kernel,
        grid=(nj, m // tm),
        in_specs=[pl.BlockSpec((tm, D_MODEL), lambda j, i: (i, 0)),
                  wspec(0), wspec(1), wspec(2), wspec(3),
                  pl.BlockSpec((CONV_WIDTH, PROJ_TN), lambda j, i: (0, j)),
                  pl.BlockSpec((1, PROJ_TN), lambda j, i: (0, j))],
        out_specs=pl.BlockSpec((tm, PROJ_TN), lambda j, i: (i, j)),
        out_shape=jax.ShapeDtypeStruct((m, D_CONV), BF16),
        name="proj_conv",
        scratch_shapes=[pltpu.VMEM((4, D_MODEL, PROJ_TN), BF16),
                        pltpu.VMEM((tm + 8, PROJ_TN), F32)],
        compiler_params=pltpu.CompilerParams(
            dimension_semantics=("arbitrary", "arbitrary"), vmem_limit_bytes=VMEM_LIMIT),
    )(h, w_in, w_in, w_in, w_in, conv_w_l, conv_gain_l.reshape(1, D_CONV))


def _strict_lower_neg(n):
    j = lax.broadcasted_iota(jnp.int32, (n, n), 0)
    s = lax.broadcasted_iota(jnp.int32, (n, n), 1)
    return jnp.where(j > s, -1.0, 0.0).astype(BF16)


def _causal(n):
    t = lax.broadcasted_iota(jnp.int32, (n, n), 0)
    s = lax.broadcasted_iota(jnp.int32, (n, n), 1)
    return s < t


def _scores(qt, kb):
    return lax.dot_general(qt, kb, (((1,), (1,)), ((), ())), preferred_element_type=F32)


def _softplus_parts(z, mask):
    sp = jnp.maximum(z, 0.0) + jnp.log(1.0 + jnp.exp(-jnp.abs(z)))
    return z - sp, (sp if mask is None else jnp.where(mask, sp, 0.0))


def _after(sp, tneg):
    return jnp.dot(sp.astype(BF16), tneg, preferred_element_type=F32)


def _pending(log_beta, sp, after, mask):
    e = log_beta + after
    if mask is not None:
        e = jnp.where(mask, e, MASKED_LOG_WEIGHT)
    return e, after[:, 0:1] - sp[:, 0:1]


def _front(qts, k_ref, heads, rows, tneg, mask):
    s = [_scores(qt, k_ref[rows, hd]) for qt, hd in zip(qts, heads)]
    parts = [_softplus_parts(si, mask) for si in s]
    after = [_after(sp, tneg) for _, sp in parts]
    return [_pending(lb, sp, af, mask) for (lb, sp), af in zip(parts, after)]


def _attn_kernel(q_ref, k_ref, v_ref, g_ref, gain_ref, o_ref, s_ref, acc_ref, run_ref, em_ref):
    tneg_kv = _strict_lower_neg(KV_BLOCK)
    tneg_meta = _strict_lower_neg(META_BLOCK)
    heads = [slice(h * HEAD_DIM, (h + 1) * HEAD_DIM) for h in range(HEADS_PER_STEP)]
    meta_rows = pl.ds(0, META_BLOCK)

    def finish(accs, rows):
        for acc, hd in zip(accs, heads):
            g = g_ref[rows, hd]
            o = acc * (g * jax.nn.sigmoid(g))
            ms = jnp.sum(o * o, axis=-1, keepdims=True) * (1.0 / HEAD_DIM)
            o_ref[rows, hd] = (o * lax.rsqrt(ms + EPS) * gain_ref[:, hd]).astype(o_ref.dtype)

    def weighted_values(ws, rows):
        return [jnp.dot(w, v_ref[rows, hd], preferred_element_type=F32)
                for w, hd in zip(ws, heads)]

    qts = [q_ref[meta_rows, hd] for hd in heads]
    pend = _front(qts, k_ref, heads, meta_rows, tneg_meta, _causal(META_BLOCK))
    finish(weighted_values([jnp.exp(e).astype(BF16) for e, _ in pend], meta_rows), meta_rows)

    diag_mask = _causal(KV_BLOCK)
    edge_mask = (lax.broadcasted_iota(jnp.int32, (Q_SUPER, KV_BLOCK), 1)
                 < lax.broadcasted_iota(jnp.int32, (Q_SUPER, KV_BLOCK), 0))
    hp = range(HEADS_PER_STEP)

    def kv_rows(j):
        return pl.ds(pl.multiple_of(META_BLOCK + j * KV_BLOCK, 128), KV_BLOCK)

    def super_tile(t, _):
        r0 = pl.multiple_of(META_BLOCK + t * Q_SUPER, 128)
        rows = pl.ds(r0, Q_SUPER)
        rows_lo = kv_rows(2 * t)
        rows_hi = kv_rows(2 * t + 1)
        q_all = [q_ref[rows, hd] for hd in heads]
        q_hi = [q_ref[rows_hi, hd] for hd in heads]

        s_hi = [_scores(q_hi[h], k_ref[rows_hi, heads[h]]) for h in hp]
        s_lo = [_scores(q_all[h], k_ref[rows_lo, heads[h]]) for h in hp]
        s_me = [_scores(q_all[h], k_ref[meta_rows, heads[h]]) for h in hp]
        first = kv_rows(jnp.maximum(2 * t - 1, 0))
        for h in hp:
            s_ref[h] = _scores(q_all[h], k_ref[first, heads[h]])
        parts_hi = [_softplus_parts(s, diag_mask) for s in s_hi]
        parts_lo = [_softplus_parts(s, edge_mask) for s in s_lo]
        parts_me = [_softplus_parts(s, None) for s in s_me]
        after_hi = [_after(sp, tneg_kv) for _, sp in parts_hi]
        after_lo = [_after(sp, tneg_kv) for _, sp in parts_lo]
        after_me = [_after(sp, tneg_meta) for _, sp in parts_me]
        for h in hp:
            e_hi, tot_hi = _pending(*parts_hi[h], after_hi[h], diag_mask)
            e_lo, tot_lo = _pending(*parts_lo[h], after_lo[h], edge_mask)
            em_ref[h] = _pending(*parts_me[h], after_me[h], None)[0]
            w_hi = jnp.exp(e_hi).astype(BF16)
            run = jnp.concatenate([jnp.zeros((KV_BLOCK, 1), F32), tot_hi], axis=0)
            w_lo = jnp.exp(e_lo + run).astype(BF16)
            run_ref[h] = jnp.broadcast_to(run + tot_lo, (Q_SUPER, LANES))
            pv_hi = jnp.dot(w_hi, v_ref[rows_hi, heads[h]], preferred_element_type=F32)
            pv_lo = jnp.dot(w_lo, v_ref[rows_lo, heads[h]], preferred_element_type=F32)
            acc_ref[h, 0:KV_BLOCK] = pv_lo[0:KV_BLOCK]
            acc_ref[h, KV_BLOCK:Q_SUPER] = pv_lo[KV_BLOCK:Q_SUPER] + pv_hi

        def add_block(parts, after, j, runs):
            out = []
            for h in hp:
                e, tot = _pending(*parts[h], after[h], None)
                w = jnp.exp(e + jnp.concatenate([runs[h]] * (KV_BLOCK // LANES), axis=1)
                            ).astype(BF16)
                out.append(runs[h] + tot)
                acc_ref[h] += jnp.dot(w, v_ref[kv_rows(j), heads[h]],
                                      preferred_element_type=F32)
            return out

        def two_blocks(n, _):
            j = 2 * t - 1 - 2 * n
            parts_0 = [_softplus_parts(s_ref[h], None) for h in hp]
            s_1 = [_scores(q_ref[rows, heads[h]], k_ref[kv_rows(j - 1), heads[h]]) for h in hp]
            after_0 = [_after(sp, tneg_kv) for _, sp in parts_0]
            parts_1 = [_softplus_parts(s, None) for s in s_1]
            runs = add_block(parts_0, after_0, j, [run_ref[h] for h in hp])
            nxt = kv_rows(jnp.maximum(j - 2, 0))
            for h in hp:
                s_ref[h] = _scores(q_ref[rows, heads[h]], k_ref[nxt, heads[h]])
            after_1 = [_after(sp, tneg_kv) for _, sp in parts_1]
            runs = add_block(parts_1, after_1, j - 1, runs)
            for h in hp:
                run_ref[h] = runs[h]
            return 0

        lax.fori_loop(0, t, two_blocks, 0)
        accs = []
        for h in hp:
            w = jnp.exp(em_ref[h] + run_ref[h]).astype(BF16)
            accs.append(acc_ref[h] + jnp.dot(w, v_ref[meta_rows, heads[h]],
                                             preferred_element_type=F32))
        finish(accs, rows)
        return 0

    lax.fori_loop(0, SEQ // Q_SUPER, super_tile, 0)


def _attention(q, k, v, g, gain_l, batch):
    m = q.shape[0]
    width = HEADS_PER_STEP * HEAD_DIM
    spec = pl.BlockSpec((L_PAD, width), lambda b, h: (b, h))
    return pl.pallas_call(
        _attn_kernel,
        grid=(batch, ATTN_HEADS // HEADS_PER_STEP),
        in_specs=[spec, spec, spec, spec,
                  pl.BlockSpec((1, width), lambda b, h: (0, h))],
        out_specs=spec,
        out_shape=jax.ShapeDtypeStruct((m, D_ATTN), BF16),
        scratch_shapes=[pltpu.VMEM((HEADS_PER_STEP, Q_SUPER, KV_BLOCK), F32),
                        pltpu.VMEM((HEADS_PER_STEP, Q_SUPER, HEAD_DIM), F32),
                        pltpu.VMEM((HEADS_PER_STEP, Q_SUPER, LANES), F32),
                        pltpu.VMEM((HEADS_PER_STEP, Q_SUPER, META_BLOCK), F32)],
        name="sb_attention",
        compiler_params=pltpu.CompilerParams(
            dimension_semantics=("arbitrary", "arbitrary"), vmem_limit_bytes=VMEM_LIMIT),
    )(q, k, v, g, gain_l.reshape(1, D_ATTN))


def _out_proj_kernel(o_ref, y_ref, w_ref, x_ref, out_ref):
    mix = jnp.concatenate([o_ref[...], y_ref[...]], axis=1)
    out_ref[...] = x_ref[...] + jnp.dot(mix, w_ref[...], preferred_element_type=F32)


def _out_proj(o, y, w_out, layer, x2d, tm):
    m = o.shape[0]
    nj = D_MODEL // OUT_TN
    return pl.pallas_call(
        _out_proj_kernel,
        grid=(m // tm, nj),
        in_specs=[pl.BlockSpec((tm, D_ATTN), lambda i, j: (i, 0)),
                  pl.BlockSpec((tm, D_CONV), lambda i, j: (i, 0)),
                  pl.BlockSpec((None, D_MODEL, OUT_TN), lambda i, j: (layer, 0, j)),
                  pl.BlockSpec((tm, OUT_TN), lambda i, j: (i, j))],
        out_specs=pl.BlockSpec((tm, OUT_TN), lambda i, j: (i, j)),
        out_shape=jax.ShapeDtypeStruct((m, D_MODEL), F32),
        name="out_proj",
        compiler_params=pltpu.CompilerParams(
            dimension_semantics=("arbitrary", "arbitrary"), vmem_limit_bytes=VMEM_LIMIT),
    )(o, y, w_out, x2d)


def kernel(x, meta_tokens, norm_g, w_in, conv_w, attn_norm_g, conv_norm_g, w_out, final_norm_g):
    batch = x.shape[0]
    depth = w_in.shape[0]
    zeros = jnp.zeros((batch, META_BLOCK - N_META, D_MODEL), x.dtype)
    meta = jnp.broadcast_to(meta_tokens.astype(x.dtype)[None], (batch, N_META, D_MODEL))
    hs = jnp.concatenate([zeros, meta, x], axis=1).reshape(batch * L_PAD, D_MODEL)
    assert hs.shape[0] % PROJ_TM == 0 and hs.shape[0] % OUT_TM == 0
    assert PROJ_TM % CONV_CHUNK == 0
    w_out_bf = w_out.astype(BF16)
    for l in range(depth):
        h = _rmsnorm(hs, norm_g[l], BF16, NORM_TM)
        q, k, v, g = _proj_attn(h, w_in, l, PROJ_TM)
        y = _proj_conv(h, w_in, l, conv_w[l], conv_norm_g[l], PROJ_TM)
        o = _attention(q, k, v, g, attn_norm_g[l], batch)
        hs = _out_proj(o, y, w_out_bf, l, hs, OUT_TM)
    return _final_norm(hs.reshape(batch, L_PAD, D_MODEL), final_norm_g)
```

```python
import math

import jax
import jax.numpy as jnp
from jax import lax
from jax.experimental import pallas as pl
from jax.experimental.pallas import tpu as pltpu

D_MODEL = 4096
SEQ = 4096
N_META = 16
HEAD_DIM = 128
ATTN_HEADS = 16
D_ATTN = ATTN_HEADS * HEAD_DIM
D_CONV = D_MODEL - D_ATTN
CONV_WIDTH = 3
EPS = 1e-6

LANES = 128
META_BLOCK = 128
L_PAD = META_BLOCK + SEQ
KV_BLOCK = 256
Q_SUPER = 2 * KV_BLOCK
LOOP_BLOCKS = 4
HEADS_PER_STEP = 2
MASKED_LOG_WEIGHT = -1e30
PROJ_TN = 256
PROJ_TM = 528
CONV_CHUNK = 176
OUT_TN = 512
OUT_TM = 1056
NORM_TM = 256
VMEM_LIMIT = 60 * 1024 * 1024

F32 = jnp.float32
BF16 = jnp.bfloat16


def _rmsnorm_kernel(x_ref, g_ref, o_ref):
    x = x_ref[...]
    ms = jnp.sum(x * x, axis=-1, keepdims=True) * (1.0 / D_MODEL)
    o_ref[...] = (x * lax.rsqrt(ms + EPS) * g_ref[...]).astype(o_ref.dtype)


def _rmsnorm(x2d, g, out_dtype, tm):
    m = x2d.shape[0]
    return pl.pallas_call(
        _rmsnorm_kernel,
        grid=(m // tm,),
        in_specs=[pl.BlockSpec((tm, D_MODEL), lambda i: (i, 0)),
                  pl.BlockSpec((1, D_MODEL), lambda i: (0, 0))],
        out_specs=pl.BlockSpec((tm, D_MODEL), lambda i: (i, 0)),
        out_shape=jax.ShapeDtypeStruct((m, D_MODEL), out_dtype),
        name="rmsnorm",
        compiler_params=pltpu.CompilerParams(
            dimension_semantics=("arbitrary",), vmem_limit_bytes=VMEM_LIMIT),
    )(x2d, g.reshape(1, D_MODEL))


def _embed_kernel(x_ref, meta_ref, g_ref, hs_ref, h_ref):
    first = jnp.concatenate(
        [jnp.zeros((META_BLOCK - N_META, D_MODEL), F32), meta_ref[...]], axis=0)
    tile = jnp.where(pl.program_id(1) == 0, first, x_ref[0])
    ms = jnp.sum(tile * tile, axis=-1, keepdims=True) * (1.0 / D_MODEL)
    hs_ref[0] = tile
    h_ref[0] = (tile * lax.rsqrt(ms + EPS) * g_ref[...]).astype(BF16)


def _embed(x, meta_tokens, g):
    b = x.shape[0]
    nblk = L_PAD // META_BLOCK
    blk = pl.BlockSpec((1, META_BLOCK, D_MODEL), lambda bi, i: (bi, i, 0))
    hs, h = pl.pallas_call(
        _embed_kernel,
        grid=(b, nblk),
        in_specs=[pl.BlockSpec((1, META_BLOCK, D_MODEL),
                               lambda bi, i: (bi, jnp.maximum(i - 1, 0), 0)),
                  pl.BlockSpec((N_META, D_MODEL), lambda bi, i: (0, 0)),
                  pl.BlockSpec((1, D_MODEL), lambda bi, i: (0, 0))],
        out_specs=[blk, blk],
        out_shape=[jax.ShapeDtypeStruct((b, L_PAD, D_MODEL), F32),
                   jax.ShapeDtypeStruct((b, L_PAD, D_MODEL), BF16)],
        name="embed_norm",
        compiler_params=pltpu.CompilerParams(
            dimension_semantics=("arbitrary", "arbitrary"), vmem_limit_bytes=VMEM_LIMIT),
    )(x, meta_tokens.astype(x.dtype), g.reshape(1, D_MODEL))
    return hs.reshape(b * L_PAD, D_MODEL), h.reshape(b * L_PAD, D_MODEL)


def _final_norm_kernel(x_ref, g_ref, o_ref):
    x = x_ref[0]
    ms = jnp.sum(x * x, axis=-1, keepdims=True) * (1.0 / D_MODEL)
    o_ref[0] = x * lax.rsqrt(ms + EPS) * g_ref[...]


def _final_norm(hs3d, g):
    b = hs3d.shape[0]
    off = META_BLOCK // 128
    return pl.pallas_call(
        _final_norm_kernel,
        grid=(b, SEQ // 128),
        in_specs=[pl.BlockSpec((1, 128, D_MODEL), lambda bi, i: (bi, i + off, 0)),
                  pl.BlockSpec((1, D_MODEL), lambda bi, i: (0, 0))],
        out_specs=pl.BlockSpec((1, 128, D_MODEL), lambda bi, i: (bi, i, 0)),
        out_shape=jax.ShapeDtypeStruct((b, SEQ, D_MODEL), F32),
        name="final_norm",
        compiler_params=pltpu.CompilerParams(
            dimension_semantics=("arbitrary", "arbitrary"), vmem_limit_bytes=VMEM_LIMIT),
    )(hs3d, g.reshape(1, D_MODEL))


def _cast_weights_once(w_refs, wbf_ref):
    @pl.when(pl.program_id(1) == 0)
    def _():
        for seg, w_ref in enumerate(w_refs):
            wbf_ref[seg] = w_ref[...].astype(BF16)


def _proj_attn_kernel(h_ref, wq_ref, wk_ref, wv_ref, wg_ref, q_ref, k_ref, v_ref, g_ref,
                      wbf_ref):
    _cast_weights_once((wq_ref, wk_ref, wv_ref, wg_ref), wbf_ref)
    h = h_ref[...]
    q = jnp.dot(h, wbf_ref[0], preferred_element_type=F32)
    q_ref[...] = (q * (1.0 / math.sqrt(HEAD_DIM))).astype(BF16)
    k_ref[...] = jnp.dot(h, wbf_ref[1], preferred_element_type=F32).astype(BF16)
    v_ref[...] = jnp.dot(h, wbf_ref[2], preferred_element_type=F32).astype(BF16)
    g_ref[...] = jnp.dot(h, wbf_ref[3], preferred_element_type=F32)


def _proj_attn(h, w_in, layer, tm):
    m = h.shape[0]
    nj = D_ATTN // PROJ_TN

    def wspec(seg):
        return pl.BlockSpec((None, D_MODEL, PROJ_TN),
                            lambda j, i, seg=seg: (layer, 0, seg * nj + j))

    ospec = pl.BlockSpec((tm, PROJ_TN), lambda j, i: (i, j))
    return pl.pallas_call(
        _proj_attn_kernel,
        grid=(nj, m // tm),
        in_specs=[pl.BlockSpec((tm, D_MODEL), lambda j, i: (i, 0)),
                  wspec(0), wspec(1), wspec(2), wspec(3)],
        out_specs=[ospec, ospec, ospec, ospec],
        out_shape=[jax.ShapeDtypeStruct((m, D_ATTN), BF16)] * 3
        + [jax.ShapeDtypeStruct((m, D_ATTN), F32)],
        scratch_shapes=[pltpu.VMEM((4, D_MODEL, PROJ_TN), BF16)],
        name="proj_attn",
        compiler_params=pltpu.CompilerParams(
            dimension_semantics=("arbitrary", "arbitrary"), vmem_limit_bytes=VMEM_LIMIT),
    )(h, w_in, w_in, w_in, w_in)


def _proj_conv_kernel(h_ref, wb_ref, wc_ref, wh_ref, wz_ref, cw_ref, gain_ref, y_ref,
                      wbf_ref, u_ref):
    i = pl.program_id(1)
    tm = h_ref.shape[0]
    _cast_weights_once((wb_ref, wc_ref, wh_ref, wz_ref), wbf_ref)

    @pl.when(i == 0)
    def _():
        u_ref[0:8, :] = jnp.zeros((8, PROJ_TN), F32)

    @pl.when(i > 0)
    def _():
        u_ref[0:8, :] = u_ref[tm:tm + 8, :]

    cw = cw_ref[...]
    gain = gain_ref[...]
    for r0 in range(0, tm, CONV_CHUNK):
        h = h_ref[r0:r0 + CONV_CHUNK, :]
        u_ref[8 + r0:8 + r0 + CONV_CHUNK, :] = (
            jnp.dot(h, wbf_ref[1], preferred_element_type=F32)
            * jnp.dot(h, wbf_ref[2], preferred_element_type=F32))
        conv = u_ref[6 + r0:6 + r0 + CONV_CHUNK, :] * cw[0:1, :]
        conv = conv + u_ref[7 + r0:7 + r0 + CONV_CHUNK, :] * cw[1:2, :]
        conv = conv + u_ref[8 + r0:8 + r0 + CONV_CHUNK, :] * cw[2:3, :]
        y = jnp.dot(h, wbf_ref[0], preferred_element_type=F32) * conv
        z = jnp.dot(h, wbf_ref[3], preferred_element_type=F32)
        y = y * (z * jax.nn.sigmoid(z))
        for c in range(PROJ_TN // LANES):
            sl = slice(c * LANES, (c + 1) * LANES)
            yc = y[:, sl]
            ms = jnp.sum(yc * yc, axis=-1, keepdims=True) * (1.0 / LANES)
            y_ref[r0:r0 + CONV_CHUNK, sl] = (yc * lax.rsqrt(ms + EPS) * gain[:, sl]).astype(BF16)


def _proj_conv(h, w_in, layer, conv_w_l, conv_gain_l, tm):
    m = h.shape[0]
    nj = D_CONV // PROJ_TN
    base = 4 * D_ATTN // PROJ_TN

    def wspec(seg):
        return pl.BlockSpec((None, D_MODEL, PROJ_TN),
                            lambda j, i, seg=seg: (layer, 0, base + seg * nj + j))

    return pl.pallas_call(
        _proj_conv_kernel,
        grid=(nj, m // tm),
        in_specs=[pl.BlockSpec((tm, D_MODEL), lambda j, i: (i, 0)),
                  wspec(0), wspec(1), wspec(2), wspec(3),
                  pl.BlockSpec((CONV_WIDTH, PROJ_TN), lambda j, i: (0, j)),
                  pl.BlockSpec((1, PROJ_TN), lambda j, i: (0, j))],
        out_specs=pl.BlockSpec((tm, PROJ_TN), lambda j, i: (i, j)),
        out_shape=jax.ShapeDtypeStruct((m, D_CONV), BF16),
        name="proj_conv",
        scratch_shapes=[pltpu.VMEM((4, D_MODEL, PROJ_TN), BF16),
                        pltpu.VMEM((tm + 8, PROJ_TN), F32)],
        compiler_params=pltpu.CompilerParams(
            dimension_semantics=("arbitrary", "arbitrary"), vmem_limit_bytes=VMEM_LIMIT),
    )(h, w_in, w_in, w_in, w_in, conv_w_l, conv_gain_l.reshape(1, D_CONV))


def _strict_lower_neg(n):
    j = lax.broadcasted_iota(jnp.int32, (n, n), 0)
    s = lax.broadcasted_iota(jnp.int32, (n, n), 1)
    return jnp.where(j > s, -1.0, 0.0).astype(BF16)


def _causal(n):
    t = lax.broadcasted_iota(jnp.int32, (n, n), 0)
    s = lax.broadcasted_iota(jnp.int32, (n, n), 1)
    return s < t


def _scores(qt, kb):
    return lax.dot_general(qt, kb, (((1,), (1,)), ((), ())), preferred_element_type=F32)


def _softplus_parts(z, mask):
    sp = jnp.maximum(z, 0.0) + jnp.log(1.0 + jnp.exp(-jnp.abs(z)))
    return z - sp, (sp if mask is None else jnp.where(mask, sp, 0.0))


def _after(sp, tneg):
    return jnp.dot(sp.astype(BF16), tneg, preferred_element_type=F32)


def _pending(log_beta, sp, after, mask):
    e = log_beta + after
    if mask is not None:
        e = jnp.where(mask, e, MASKED_LOG_WEIGHT)
    return e, after[:, 0:1] - sp[:, 0:1]


def _front(qts, k_ref, heads, rows, tneg, mask):
    s = [_scores(qt, k_ref[rows, hd]) for qt, hd in zip(qts, heads)]
    parts = [_softplus_parts(si, mask) for si in s]
    after = [_after(sp, tneg) for _, sp in parts]
    return [_pending(lb, sp, af, mask) for (lb, sp), af in zip(parts, after)]


def _attn_kernel(q_ref, k_ref, v_ref, g_ref, gain_ref, o_ref, s_ref, acc_ref, run_ref, em_ref):
    tneg_kv = _strict_lower_neg(KV_BLOCK)
    tneg_meta = _strict_lower_neg(META_BLOCK)
    heads = [slice(h * HEAD_DIM, (h + 1) * HEAD_DIM) for h in range(HEADS_PER_STEP)]
    meta_rows = pl.ds(0, META_BLOCK)

    def finish(accs, rows):
        for acc, hd in zip(accs, heads):
            g = g_ref[rows, hd]
            o = acc * (g * jax.nn.sigmoid(g))
            ms = jnp.sum(o * o, axis=-1, keepdims=True) * (1.0 / HEAD_DIM)
            o_ref[rows, hd] = (o * lax.rsqrt(ms + EPS) * gain_ref[:, hd]).astype(o_ref.dtype)

    def weighted_values(ws, rows):
        return [jnp.dot(w, v_ref[rows, hd], preferred_element_type=F32)
                for w, hd in zip(ws, heads)]

    qts = [q_ref[meta_rows, hd] for hd in heads]
    pend = _front(qts, k_ref, heads, meta_rows, tneg_meta, _causal(META_BLOCK))
    finish(weighted_values([jnp.exp(e).astype(BF16) for e, _ in pend], meta_rows), meta_rows)

    diag_mask = _causal(KV_BLOCK)
    edge_mask = (lax.broadcasted_iota(jnp.int32, (Q_SUPER, KV_BLOCK), 1)
                 < lax.broadcasted_iota(jnp.int32, (Q_SUPER, KV_BLOCK), 0))
    hp = range(HEADS_PER_STEP)

    def kv_rows(j):
        return pl.ds(pl.multiple_of(META_BLOCK + j * KV_BLOCK, 128), KV_BLOCK)

    def super_tile(t, _):
        r0 = pl.multiple_of(META_BLOCK + t * Q_SUPER, 128)
        rows = pl.ds(r0, Q_SUPER)
        rows_lo = kv_rows(2 * t)
        rows_hi = kv_rows(2 * t + 1)
        q_all = [q_ref[rows, hd] for hd in heads]
        q_hi = [q_ref[rows_hi, hd] for hd in heads]

        s_hi = [_scores(q_hi[h], k_ref[rows_hi, heads[h]]) for h in hp]
        s_lo = [_scores(q_all[h], k_ref[rows_lo, heads[h]]) for h in hp]
        s_me = [_scores(q_all[h], k_ref[meta_rows, heads[h]]) for h in hp]
        first = kv_rows(jnp.maximum(2 * t - 1, 0))
        for h in hp:
            s_ref[h] = _scores(q_all[h], k_ref[first, heads[h]])
        parts_hi = [_softplus_parts(s, diag_mask) for s in s_hi]
        parts_lo = [_softplus_parts(s, edge_mask) for s in s_lo]
        parts_me = [_softplus_parts(s, None) for s in s_me]
        after_hi = [_after(sp, tneg_kv) for _, sp in parts_hi]
        after_lo = [_after(sp, tneg_kv) for _, sp in parts_lo]
        after_me = [_after(sp, tneg_meta) for _, sp in parts_me]
        for h in hp:
            e_hi, tot_hi = _pending(*parts_hi[h], after_hi[h], diag_mask)
            e_lo, tot_lo = _pending(*parts_lo[h], after_lo[h], edge_mask)
            em_ref[h] = _pending(*parts_me[h], after_me[h], None)[0]
            w_hi = jnp.exp(e_hi).astype(BF16)
            run = jnp.concatenate([jnp.zeros((KV_BLOCK, 1), F32), tot_hi], axis=0)
            w_lo = jnp.exp(e_lo + run).astype(BF16)
            run_ref[h] = jnp.broadcast_to(run + tot_lo, (Q_SUPER, LANES))
            pv_hi = jnp.dot(w_hi, v_ref[rows_hi, heads[h]], preferred_element_type=F32)
            pv_lo = jnp.dot(w_lo, v_ref[rows_lo, heads[h]], preferred_element_type=F32)
            acc_ref[h, 0:KV_BLOCK] = pv_lo[0:KV_BLOCK]
            acc_ref[h, KV_BLOCK:Q_SUPER] = pv_lo[KV_BLOCK:Q_SUPER] + pv_hi

        def add_block(parts, after, j, runs):
            out = []
            for h in hp:
                e, tot = _pending(*parts[h], after[h], None)
                w = jnp.exp(e + jnp.concatenate([runs[h]] * (KV_BLOCK // LANES), axis=1)
                            ).astype(BF16)
                out.append(runs[h] + tot)
                acc_ref[h] += jnp.dot(w, v_ref[kv_rows(j), heads[h]],
                                      preferred_element_type=F32)
            return out

        def sweep(j0, count):
            s = [s_ref[h] for h in hp]
            runs = [run_ref[h] for h in hp]
            for u in range(count):
                parts = [_softplus_parts(si, None) for si in s]
                nxt = kv_rows(jnp.maximum(j0 - u - 1, 0))
                s = [_scores(q_ref[rows, heads[h]], k_ref[nxt, heads[h]]) for h in hp]
                after = [_after(sp, tneg_kv) for _, sp in parts]
                runs = add_block(parts, after, j0 - u, runs)
            for h in hp:
                s_ref[h] = s[h]
                run_ref[h] = runs[h]
            return 0

        lax.fori_loop(0, t % 2, lambda n, c: sweep(2 * t - 1, 2), 0)
        base = 2 * t - 1 - 2 * (t % 2)
        lax.fori_loop(0, t // 2, lambda n, c: sweep(base - LOOP_BLOCKS * n, LOOP_BLOCKS), 0)
        accs = []
        for h in hp:
            w = jnp.exp(em_ref[h] + run_ref[h]).astype(BF16)
            accs.append(acc_ref[h] + jnp.dot(w, v_ref[meta_rows, heads[h]],
                                             preferred_element_type=F32))
        finish(accs, rows)
        return 0

    lax.fori_loop(0, SEQ // Q_SUPER, super_tile, 0)


def _attention(q, k, v, g, gain_l, batch):
    m = q.shape[0]
    width = HEADS_PER_STEP * HEAD_DIM
    spec = pl.BlockSpec((L_PAD, width), lambda b, h: (b, h))
    return pl.pallas_call(
        _attn_kernel,
        grid=(batch, ATTN_HEADS // HEADS_PER_STEP),
        in_specs=[spec, spec, spec, spec,
                  pl.BlockSpec((1, width), lambda b, h: (0, h))],
        out_specs=spec,
        out_shape=jax.ShapeDtypeStruct((m, D_ATTN), BF16),
        scratch_shapes=[pltpu.VMEM((HEADS_PER_STEP, Q_SUPER, KV_BLOCK), F32),
                        pltpu.VMEM((HEADS_PER_STEP, Q_SUPER, HEAD_DIM), F32),
                        pltpu.VMEM((HEADS_PER_STEP, Q_SUPER, LANES), F32),
                        pltpu.VMEM((HEADS_PER_STEP, Q_SUPER, META_BLOCK), F32)],
        name="sb_attention",
        compiler_params=pltpu.CompilerParams(
            dimension_semantics=("arbitrary", "arbitrary"), vmem_limit_bytes=VMEM_LIMIT),
    )(q, k, v, g, gain_l.reshape(1, D_ATTN))


def _out_proj_kernel(o_ref, y_ref, w_ref, x_ref, out_ref, wbf_ref):
    _cast_weights_once((w_ref,), wbf_ref)
    mix = jnp.concatenate([o_ref[...], y_ref[...]], axis=1)
    out_ref[...] = x_ref[...] + jnp.dot(mix, wbf_ref[0], preferred_element_type=F32)


def _out_proj(o, y, w_out, layer, x2d, tm):
    m = o.shape[0]
    nj = D_MODEL // OUT_TN
    return pl.pallas_call(
        _out_proj_kernel,
        grid=(nj, m // tm),
        in_specs=[pl.BlockSpec((tm, D_ATTN), lambda j, i: (i, 0)),
                  pl.BlockSpec((tm, D_CONV), lambda j, i: (i, 0)),
                  pl.BlockSpec((None, D_MODEL, OUT_TN), lambda j, i: (layer, 0, j)),
                  pl.BlockSpec((tm, OUT_TN), lambda j, i: (i, j))],
        out_specs=pl.BlockSpec((tm, OUT_TN), lambda j, i: (i, j)),
        out_shape=jax.ShapeDtypeStruct((m, D_MODEL), F32),
        scratch_shapes=[pltpu.VMEM((1, D_MODEL, OUT_TN), BF16)],
        name="out_proj",
        compiler_params=pltpu.CompilerParams(
            dimension_semantics=("arbitrary", "arbitrary"), vmem_limit_bytes=VMEM_LIMIT),
    )(o, y, w_out, x2d)


def kernel(x, meta_tokens, norm_g, w_in, conv_w, attn_norm_g, conv_norm_g, w_out, final_norm_g):
    batch = x.shape[0]
    depth = w_in.shape[0]
    hs, h = _embed(x, meta_tokens, norm_g[0])
    assert hs.shape[0] % PROJ_TM == 0 and hs.shape[0] % OUT_TM == 0
    assert PROJ_TM % CONV_CHUNK == 0
    for l in range(depth):
        if l > 0:
            h = _rmsnorm(hs, norm_g[l], BF16, NORM_TM)
        q, k, v, g = _proj_attn(h, w_in, l, PROJ_TM)
        y = _proj_conv(h, w_in, l, conv_w[l], conv_norm_g[l], PROJ_TM)
        o = _attention(q, k, v, g, attn_norm_g[l], batch)
        hs = _out_proj(o, y, w_out, l, hs, OUT_TM)
    return _final_norm(hs.reshape(batch, L_PAD, D_MODEL), final_norm_g)
```

```python
import math

import jax
import jax.numpy as jnp
from jax import lax
from jax.experimental import pallas as pl
from jax.experimental.pallas import tpu as pltpu

D_MODEL = 4096
SEQ = 4096
N_META = 16
HEAD_DIM = 128
ATTN_HEADS = 16
D_ATTN = ATTN_HEADS * HEAD_DIM
D_CONV = D_MODEL - D_ATTN
CONV_WIDTH = 3
EPS = 1e-6

LANES = 128
META_BLOCK = 128
L_PAD = META_BLOCK + SEQ
KV_BLOCK = 256
Q_SUPER = 2 * KV_BLOCK
LOOP_BLOCKS = 4
HEADS_PER_STEP = 2
MASKED_LOG_WEIGHT = -1e30
PROJ_TN = 256
PROJ_TM = 528
CONV_CHUNK = 176
OUT_TN = 512
OUT_TM = 1056
NORM_TM = 256
VMEM_LIMIT = 60 * 1024 * 1024

F32 = jnp.float32
BF16 = jnp.bfloat16


def _rmsnorm_kernel(x_ref, g_ref, o_ref):
    x = x_ref[...]
    ms = jnp.sum(x * x, axis=-1, keepdims=True) * (1.0 / D_MODEL)
    o_ref[...] = (x * lax.rsqrt(ms + EPS) * g_ref[...]).astype(o_ref.dtype)


def _rmsnorm(x2d, g, out_dtype, tm):
    m = x2d.shape[0]
    return pl.pallas_call(
        _rmsnorm_kernel,
        grid=(m // tm,),
        in_specs=[pl.BlockSpec((tm, D_MODEL), lambda i: (i, 0)),
                  pl.BlockSpec((1, D_MODEL), lambda i: (0, 0))],
        out_specs=pl.BlockSpec((tm, D_MODEL), lambda i: (i, 0)),
        out_shape=jax.ShapeDtypeStruct((m, D_MODEL), out_dtype),
        name="rmsnorm",
        compiler_params=pltpu.CompilerParams(
            dimension_semantics=("arbitrary",), vmem_limit_bytes=VMEM_LIMIT),
    )(x2d, g.reshape(1, D_MODEL))


def _embed_kernel(x_ref, meta_ref, g_ref, hs_ref, h_ref):
    first = jnp.concatenate(
        [jnp.zeros((META_BLOCK - N_META, D_MODEL), F32), meta_ref[...]], axis=0)
    tile = jnp.where(pl.program_id(1) == 0, first, x_ref[0])
    ms = jnp.sum(tile * tile, axis=-1, keepdims=True) * (1.0 / D_MODEL)
    hs_ref[0] = tile
    h_ref[0] = (tile * lax.rsqrt(ms + EPS) * g_ref[...]).astype(BF16)


def _embed(x, meta_tokens, g):
    b = x.shape[0]
    nblk = L_PAD // META_BLOCK
    blk = pl.BlockSpec((1, META_BLOCK, D_MODEL), lambda bi, i: (bi, i, 0))
    hs, h = pl.pallas_call(
        _embed_kernel,
        grid=(b, nblk),
        in_specs=[pl.BlockSpec((1, META_BLOCK, D_MODEL),
                               lambda bi, i: (bi, jnp.maximum(i - 1, 0), 0)),
                  pl.BlockSpec((N_META, D_MODEL), lambda bi, i: (0, 0)),
                  pl.BlockSpec((1, D_MODEL), lambda bi, i: (0, 0))],
        out_specs=[blk, blk],
        out_shape=[jax.ShapeDtypeStruct((b, L_PAD, D_MODEL), F32),
                   jax.ShapeDtypeStruct((b, L_PAD, D_MODEL), BF16)],
        name="embed_norm",
        compiler_params=pltpu.CompilerParams(
            dimension_semantics=("arbitrary", "arbitrary"), vmem_limit_bytes=VMEM_LIMIT),
    )(x, meta_tokens.astype(x.dtype), g.reshape(1, D_MODEL))
    return hs.reshape(b * L_PAD, D_MODEL), h.reshape(b * L_PAD, D_MODEL)


def _final_norm_kernel(x_ref, g_ref, o_ref):
    x = x_ref[0]
    ms = jnp.sum(x * x, axis=-1, keepdims=True) * (1.0 / D_MODEL)
    o_ref[0] = x * lax.rsqrt(ms + EPS) * g_ref[...]


def _final_norm(hs3d, g):
    b = hs3d.shape[0]
    off = META_BLOCK // 128
    return pl.pallas_call(
        _final_norm_kernel,
        grid=(b, SEQ // 128),
        in_specs=[pl.BlockSpec((1, 128, D_MODEL), lambda bi, i: (bi, i + off, 0)),
                  pl.BlockSpec((1, D_MODEL), lambda bi, i: (0, 0))],
        out_specs=pl.BlockSpec((1, 128, D_MODEL), lambda bi, i: (bi, i, 0)),
        out_shape=jax.ShapeDtypeStruct((b, SEQ, D_MODEL), F32),
        name="final_norm",
        compiler_params=pltpu.CompilerParams(
            dimension_semantics=("arbitrary", "arbitrary"), vmem_limit_bytes=VMEM_LIMIT),
    )(hs3d, g.reshape(1, D_MODEL))


def _cast_weights_once(w_refs, wbf_ref):
    @pl.when(pl.program_id(1) == 0)
    def _():
        for seg, w_ref in enumerate(w_refs):
            wbf_ref[seg] = w_ref[...].astype(BF16)


def _proj_attn_kernel(h_ref, wq_ref, wk_ref, wv_ref, wg_ref, q_ref, k_ref, v_ref, g_ref,
                      wbf_ref):
    _cast_weights_once((wq_ref, wk_ref, wv_ref, wg_ref), wbf_ref)
    h = h_ref[...]
    q = jnp.dot(h, wbf_ref[0], preferred_element_type=F32)
    q_ref[...] = (q * (1.0 / math.sqrt(HEAD_DIM))).astype(BF16)
    k_ref[...] = jnp.dot(h, wbf_ref[1], preferred_element_type=F32).astype(BF16)
    v_ref[...] = jnp.dot(h, wbf_ref[2], preferred_element_type=F32).astype(BF16)
    g_ref[...] = jnp.dot(h, wbf_ref[3], preferred_element_type=F32)


def _proj_attn(h, w_in, layer, tm):
    m = h.shape[0]
    nj = D_ATTN // PROJ_TN

    def wspec(seg):
        return pl.BlockSpec((None, D_MODEL, PROJ_TN),
                            lambda j, i, seg=seg: (layer, 0, seg * nj + j))

    ospec = pl.BlockSpec((tm, PROJ_TN), lambda j, i: (i, j))
    return pl.pallas_call(
        _proj_attn_kernel,
        grid=(nj, m // tm),
        in_specs=[pl.BlockSpec((tm, D_MODEL), lambda j, i: (i, 0)),
                  wspec(0), wspec(1), wspec(2), wspec(3)],
        out_specs=[ospec, ospec, ospec, ospec],
        out_shape=[jax.ShapeDtypeStruct((m, D_ATTN), BF16)] * 3
        + [jax.ShapeDtypeStruct((m, D_ATTN), F32)],
        scratch_shapes=[pltpu.VMEM((4, D_MODEL, PROJ_TN), BF16)],
        name="proj_attn",
        compiler_params=pltpu.CompilerParams(
            dimension_semantics=("arbitrary", "arbitrary"), vmem_limit_bytes=VMEM_LIMIT),
    )(h, w_in, w_in, w_in, w_in)


def _proj_conv_kernel(h_ref, wb_ref, wc_ref, wh_ref, wz_ref, cw_ref, gain_ref, y_ref,
                      wbf_ref, u_ref):
    i = pl.program_id(1)
    tm = h_ref.shape[0]
    _cast_weights_once((wb_ref, wc_ref, wh_ref, wz_ref), wbf_ref)

    @pl.when(i == 0)
    def _():
        u_ref[0:8, :] = jnp.zeros((8, PROJ_TN), F32)

    @pl.when(i > 0)
    def _():
        u_ref[0:8, :] = u_ref[tm:tm + 8, :]

    cw = cw_ref[...]
    gain = gain_ref[...]
    for r0 in range(0, tm, CONV_CHUNK):
        h = h_ref[r0:r0 + CONV_CHUNK, :]
        u_ref[8 + r0:8 + r0 + CONV_CHUNK, :] = (
            jnp.dot(h, wbf_ref[1], preferred_element_type=F32)
            * jnp.dot(h, wbf_ref[2], preferred_element_type=F32))
        conv = u_ref[6 + r0:6 + r0 + CONV_CHUNK, :] * cw[0:1, :]
        conv = conv + u_ref[7 + r0:7 + r0 + CONV_CHUNK, :] * cw[1:2, :]
        conv = conv + u_ref[8 + r0:8 + r0 + CONV_CHUNK, :] * cw[2:3, :]
        y = jnp.dot(h, wbf_ref[0], preferred_element_type=F32) * conv
        z = jnp.dot(h, wbf_ref[3], preferred_element_type=F32)
        y = y * (z * jax.nn.sigmoid(z))
        for c in range(PROJ_TN // LANES):
            sl = slice(c * LANES, (c + 1) * LANES)
            yc = y[:, sl]
            ms = jnp.sum(yc * yc, axis=-1, keepdims=True) * (1.0 / LANES)
            y_ref[r0:r0 + CONV_CHUNK, sl] = (yc * lax.rsqrt(ms + EPS) * gain[:, sl]).astype(BF16)


def _proj_conv(h, w_in, layer, conv_w_l, conv_gain_l, tm):
    m = h.shape[0]
    nj = D_CONV // PROJ_TN
    base = 4 * D_ATTN // PROJ_TN

    def wspec(seg):
        return pl.BlockSpec((None, D_MODEL, PROJ_TN),
                            lambda j, i, seg=seg: (layer, 0, base + seg * nj + j))

    return pl.pallas_call(
        _proj_conv_kernel,
        grid=(nj, m // tm),
        in_specs=[pl.BlockSpec((tm, D_MODEL), lambda j, i: (i, 0)),
                  wspec(0), wspec(1), wspec(2), wspec(3),
                  pl.BlockSpec((CONV_WIDTH, PROJ_TN), lambda j, i: (0, j)),
                  pl.BlockSpec((1, PROJ_TN), lambda j, i: (0, j))],
        out_specs=pl.BlockSpec((tm, PROJ_TN), lambda j, i: (i, j)),
        out_shape=jax.ShapeDtypeStruct((m, D_CONV), BF16),
        name="proj_conv",
        scratch_shapes=[pltpu.VMEM((4, D_MODEL, PROJ_TN), BF16),
                        pltpu.VMEM((tm + 8, PROJ_TN), F32)],
        compiler_params=pltpu.CompilerParams(
            dimension_semantics=("arbitrary", "arbitrary"), vmem_limit_bytes=VMEM_LIMIT),
    )(h, w_in, w_in, w_in, w_in, conv_w_l, conv_gain_l.reshape(1, D_CONV))


def _strict_lower_neg(n):
    j = lax.broadcasted_iota(jnp.int32, (n, n), 0)
    s = lax.broadcasted_iota(jnp.int32, (n, n), 1)
    return jnp.where(j > s, -1.0, 0.0).astype(BF16)


def _causal(n):
    t = lax.broadcasted_iota(jnp.int32, (n, n), 0)
    s = lax.broadcasted_iota(jnp.int32, (n, n), 1)
    return s < t


def _scores(qt, kb):
    return lax.dot_general(qt, kb, (((1,), (1,)), ((), ())), preferred_element_type=F32)


def _softplus_parts(z, mask):
    sp = jnp.maximum(z, 0.0) + jnp.log(1.0 + jnp.exp(-jnp.abs(z)))
    return z - sp, (sp if mask is None else jnp.where(mask, sp, 0.0))


def _after(sp, tneg):
    return jnp.dot(sp.astype(BF16), tneg, preferred_element_type=F32)


def _pending(log_beta, sp, after, mask):
    e = log_beta + after
    if mask is not None:
        e = jnp.where(mask, e, MASKED_LOG_WEIGHT)
    return e, after[:, 0:1] - sp[:, 0:1]


def _weights(log_w):
    return jnp.exp(log_w.astype(BF16))


def _front(qts, k_ref, heads, rows, tneg, mask):
    s = [_scores(qt, k_ref[rows, hd]) for qt, hd in zip(qts, heads)]
    parts = [_softplus_parts(si, mask) for si in s]
    after = [_after(sp, tneg) for _, sp in parts]
    return [_pending(lb, sp, af, mask) for (lb, sp), af in zip(parts, after)]


def _attn_kernel(q_ref, k_ref, v_ref, g_ref, gain_ref, o_ref, s_ref, acc_ref, run_ref, em_ref):
    tneg_kv = _strict_lower_neg(KV_BLOCK)
    tneg_meta = _strict_lower_neg(META_BLOCK)
    heads = [slice(h * HEAD_DIM, (h + 1) * HEAD_DIM) for h in range(HEADS_PER_STEP)]
    meta_rows = pl.ds(0, META_BLOCK)

    def finish(accs, rows):
        for acc, hd in zip(accs, heads):
            g = g_ref[rows, hd]
            o = acc * (g * jax.nn.sigmoid(g))
            ms = jnp.sum(o * o, axis=-1, keepdims=True) * (1.0 / HEAD_DIM)
            o_ref[rows, hd] = (o * lax.rsqrt(ms + EPS) * gain_ref[:, hd]).astype(o_ref.dtype)

    def weighted_values(ws, rows):
        return [jnp.dot(w, v_ref[rows, hd], preferred_element_type=F32)
                for w, hd in zip(ws, heads)]

    qts = [q_ref[meta_rows, hd] for hd in heads]
    pend = _front(qts, k_ref, heads, meta_rows, tneg_meta, _causal(META_BLOCK))
    finish(weighted_values([_weights(e) for e, _ in pend], meta_rows), meta_rows)

    diag_mask = _causal(KV_BLOCK)
    edge_mask = (lax.broadcasted_iota(jnp.int32, (Q_SUPER, KV_BLOCK), 1)
                 < lax.broadcasted_iota(jnp.int32, (Q_SUPER, KV_BLOCK), 0))
    hp = range(HEADS_PER_STEP)

    def kv_rows(j):
        return pl.ds(pl.multiple_of(META_BLOCK + j * KV_BLOCK, 128), KV_BLOCK)

    def super_tile(t, _):
        r0 = pl.multiple_of(META_BLOCK + t * Q_SUPER, 128)
        rows = pl.ds(r0, Q_SUPER)
        rows_lo = kv_rows(2 * t)
        rows_hi = kv_rows(2 * t + 1)
        q_all = [q_ref[rows, hd] for hd in heads]
        q_hi = [q_ref[rows_hi, hd] for hd in heads]

        s_hi = [_scores(q_hi[h], k_ref[rows_hi, heads[h]]) for h in hp]
        s_lo = [_scores(q_all[h], k_ref[rows_lo, heads[h]]) for h in hp]
        s_me = [_scores(q_all[h], k_ref[meta_rows, heads[h]]) for h in hp]
        first = kv_rows(jnp.maximum(2 * t - 1, 0))
        for h in hp:
            s_ref[h] = _scores(q_all[h], k_ref[first, heads[h]])
        parts_hi = [_softplus_parts(s, diag_mask) for s in s_hi]
        parts_lo = [_softplus_parts(s, edge_mask) for s in s_lo]
        parts_me = [_softplus_parts(s, None) for s in s_me]
        after_hi = [_after(sp, tneg_kv) for _, sp in parts_hi]
        after_lo = [_after(sp, tneg_kv) for _, sp in parts_lo]
        after_me = [_after(sp, tneg_meta) for _, sp in parts_me]
        for h in hp:
            e_hi, tot_hi = _pending(*parts_hi[h], after_hi[h], diag_mask)
            e_lo, tot_lo = _pending(*parts_lo[h], after_lo[h], edge_mask)
            em_ref[h] = _pending(*parts_me[h], after_me[h], None)[0]
            w_hi = _weights(e_hi)
            run = jnp.concatenate([jnp.zeros((KV_BLOCK, 1), F32), tot_hi], axis=0)
            w_lo = _weights(e_lo + run)
            run_ref[h] = jnp.broadcast_to(run + tot_lo, (Q_SUPER, LANES))
            pv_hi = jnp.dot(w_hi, v_ref[rows_hi, heads[h]], preferred_element_type=F32)
            pv_lo = jnp.dot(w_lo, v_ref[rows_lo, heads[h]], preferred_element_type=F32)
            acc_ref[h, 0:KV_BLOCK] = pv_lo[0:KV_BLOCK]
            acc_ref[h, KV_BLOCK:Q_SUPER] = pv_lo[KV_BLOCK:Q_SUPER] + pv_hi

        def add_block(parts, after, j, runs):
            out = []
            for h in hp:
                e, tot = _pending(*parts[h], after[h], None)
                w = _weights(e + jnp.concatenate([runs[h]] * (KV_BLOCK // LANES), axis=1))
                out.append(runs[h] + tot)
                acc_ref[h] += jnp.dot(w, v_ref[kv_rows(j), heads[h]],
                                      preferred_element_type=F32)
            return out

        def sweep(j0, count):
            s = [s_ref[h] for h in hp]
            runs = [run_ref[h] for h in hp]
            for u in range(count):
                parts = [_softplus_parts(si, None) for si in s]
                nxt = kv_rows(jnp.maximum(j0 - u - 1, 0))
                s = [_scores(q_ref[rows, heads[h]], k_ref[nxt, heads[h]]) for h in hp]
                after = [_after(sp, tneg_kv) for _, sp in parts]
                runs = add_block(parts, after, j0 - u, runs)
            for h in hp:
                s_ref[h] = s[h]
                run_ref[h] = runs[h]
            return 0

        lax.fori_loop(0, t % 2, lambda n, c: sweep(2 * t - 1, 2), 0)
        base = 2 * t - 1 - 2 * (t % 2)
        lax.fori_loop(0, t // 2, lambda n, c: sweep(base - LOOP_BLOCKS * n, LOOP_BLOCKS), 0)
        accs = []
        for h in hp:
            w = _weights(em_ref[h] + run_ref[h])
            accs.append(acc_ref[h] + jnp.dot(w, v_ref[meta_rows, heads[h]],
                                             preferred_element_type=F32))
        finish(accs, rows)
        return 0

    lax.fori_loop(0, SEQ // Q_SUPER, super_tile, 0)


def _attention(q, k, v, g, gain_l, batch):
    m = q.shape[0]
    width = HEADS_PER_STEP * HEAD_DIM
    spec = pl.BlockSpec((L_PAD, width), lambda b, h: (b, h))
    return pl.pallas_call(
        _attn_kernel,
        grid=(batch, ATTN_HEADS // HEADS_PER_STEP),
        in_specs=[spec, spec, spec, spec,
                  pl.BlockSpec((1, width), lambda b, h: (0, h))],
        out_specs=spec,
        out_shape=jax.ShapeDtypeStruct((m, D_ATTN), BF16),
        scratch_shapes=[pltpu.VMEM((HEADS_PER_STEP, Q_SUPER, KV_BLOCK), F32),
                        pltpu.VMEM((HEADS_PER_STEP, Q_SUPER, HEAD_DIM), F32),
                        pltpu.VMEM((HEADS_PER_STEP, Q_SUPER, LANES), F32),
                        pltpu.VMEM((HEADS_PER_STEP, Q_SUPER, META_BLOCK), F32)],
        name="sb_attention",
        compiler_params=pltpu.CompilerParams(
            dimension_semantics=("arbitrary", "arbitrary"), vmem_limit_bytes=VMEM_LIMIT),
    )(q, k, v, g, gain_l.reshape(1, D_ATTN))


def _out_proj_kernel(o_ref, y_ref, w_ref, x_ref, out_ref, wbf_ref):
    _cast_weights_once((w_ref,), wbf_ref)
    mix = jnp.concatenate([o_ref[...], y_ref[...]], axis=1)
    out_ref[...] = x_ref[...] + jnp.dot(mix, wbf_ref[0], preferred_element_type=F32)


def _out_proj(o, y, w_out, layer, x2d, tm):
    m = o.shape[0]
    nj = D_MODEL // OUT_TN
    return pl.pallas_call(
        _out_proj_kernel,
        grid=(nj, m // tm),
        in_specs=[pl.BlockSpec((tm, D_ATTN), lambda j, i: (i, 0)),
                  pl.BlockSpec((tm, D_CONV), lambda j, i: (i, 0)),
                  pl.BlockSpec((None, D_MODEL, OUT_TN), lambda j, i: (layer, 0, j)),
                  pl.BlockSpec((tm, OUT_TN), lambda j, i: (i, j))],
        out_specs=pl.BlockSpec((tm, OUT_TN), lambda j, i: (i, j)),
        out_shape=jax.ShapeDtypeStruct((m, D_MODEL), F32),
        scratch_shapes=[pltpu.VMEM((1, D_MODEL, OUT_TN), BF16)],
        name="out_proj",
        compiler_params=pltpu.CompilerParams(
            dimension_semantics=("arbitrary", "arbitrary"), vmem_limit_bytes=VMEM_LIMIT),
    )(o, y, w_out, x2d)


def kernel(x, meta_tokens, norm_g, w_in, conv_w, attn_norm_g, conv_norm_g, w_out, final_norm_g):
    batch = x.shape[0]
    depth = w_in.shape[0]
    hs, h = _embed(x, meta_tokens, norm_g[0])
    assert hs.shape[0] % PROJ_TM == 0 and hs.shape[0] % OUT_TM == 0
    assert PROJ_TM % CONV_CHUNK == 0
    for l in range(depth):
        if l > 0:
            h = _rmsnorm(hs, norm_g[l], BF16, NORM_TM)
        q, k, v, g = _proj_attn(h, w_in, l, PROJ_TM)
        y = _proj_conv(h, w_in, l, conv_w[l], conv_norm_g[l], PROJ_TM)
        o = _attention(q, k, v, g, attn_norm_g[l], batch)
        hs = _out_proj(o, y, w_out, l, hs, OUT_TM)
    return _final_norm(hs.reshape(batch, L_PAD, D_MODEL), final_norm_g)
```

```python
import math

import jax
import jax.numpy as jnp
from jax import lax
from jax.experimental import pallas as pl
from jax.experimental.pallas import tpu as pltpu

D_MODEL = 4096
SEQ = 4096
N_META = 16
HEAD_DIM = 128
ATTN_HEADS = 16
D_ATTN = ATTN_HEADS * HEAD_DIM
D_CONV = D_MODEL - D_ATTN
CONV_WIDTH = 3
EPS = 1e-6

LANES = 128
META_BLOCK = 128
L_PAD = META_BLOCK + SEQ
KV_BLOCK = 256
Q_SUPER = 2 * KV_BLOCK
LOOP_BLOCKS = 4
HEADS_PER_STEP = 2
MASKED_LOG_WEIGHT = -1e30
PROJ_TN = 256
PROJ_TM = 704
CONV_CHUNK = 176
OUT_TN = 512
OUT_TM = 1408
NORM_TM = 256
VMEM_LIMIT = 60 * 1024 * 1024

F32 = jnp.float32
BF16 = jnp.bfloat16


def _rmsnorm_kernel(x_ref, g_ref, o_ref):
    x = x_ref[...]
    ms = jnp.sum(x * x, axis=-1, keepdims=True) * (1.0 / D_MODEL)
    o_ref[...] = (x * lax.rsqrt(ms + EPS) * g_ref[...]).astype(o_ref.dtype)


def _rmsnorm(x2d, g, out_dtype, tm):
    m = x2d.shape[0]
    return pl.pallas_call(
        _rmsnorm_kernel,
        grid=(m // tm,),
        in_specs=[pl.BlockSpec((tm, D_MODEL), lambda i: (i, 0)),
                  pl.BlockSpec((1, D_MODEL), lambda i: (0, 0))],
        out_specs=pl.BlockSpec((tm, D_MODEL), lambda i: (i, 0)),
        out_shape=jax.ShapeDtypeStruct((m, D_MODEL), out_dtype),
        name="rmsnorm",
        compiler_params=pltpu.CompilerParams(
            dimension_semantics=("arbitrary",), vmem_limit_bytes=VMEM_LIMIT),
    )(x2d, g.reshape(1, D_MODEL))


def _embed_kernel(x_ref, meta_ref, g_ref, hs_ref, h_ref):
    first = jnp.concatenate(
        [jnp.zeros((META_BLOCK - N_META, D_MODEL), F32), meta_ref[...]], axis=0)
    tile = jnp.where(pl.program_id(1) == 0, first, x_ref[0])
    ms = jnp.sum(tile * tile, axis=-1, keepdims=True) * (1.0 / D_MODEL)
    hs_ref[0] = tile
    h_ref[0] = (tile * lax.rsqrt(ms + EPS) * g_ref[...]).astype(BF16)


def _embed(x, meta_tokens, g):
    b = x.shape[0]
    nblk = L_PAD // META_BLOCK
    blk = pl.BlockSpec((1, META_BLOCK, D_MODEL), lambda bi, i: (bi, i, 0))
    hs, h = pl.pallas_call(
        _embed_kernel,
        grid=(b, nblk),
        in_specs=[pl.BlockSpec((1, META_BLOCK, D_MODEL),
                               lambda bi, i: (bi, jnp.maximum(i - 1, 0), 0)),
                  pl.BlockSpec((N_META, D_MODEL), lambda bi, i: (0, 0)),
                  pl.BlockSpec((1, D_MODEL), lambda bi, i: (0, 0))],
        out_specs=[blk, blk],
        out_shape=[jax.ShapeDtypeStruct((b, L_PAD, D_MODEL), F32),
                   jax.ShapeDtypeStruct((b, L_PAD, D_MODEL), BF16)],
        name="embed_norm",
        compiler_params=pltpu.CompilerParams(
            dimension_semantics=("arbitrary", "arbitrary"), vmem_limit_bytes=VMEM_LIMIT),
    )(x, meta_tokens.astype(x.dtype), g.reshape(1, D_MODEL))
    return hs.reshape(b * L_PAD, D_MODEL), h.reshape(b * L_PAD, D_MODEL)


def _final_norm_kernel(x_ref, g_ref, o_ref):
    x = x_ref[0]
    ms = jnp.sum(x * x, axis=-1, keepdims=True) * (1.0 / D_MODEL)
    o_ref[0] = x * lax.rsqrt(ms + EPS) * g_ref[...]


def _final_norm(hs3d, g):
    b = hs3d.shape[0]
    off = META_BLOCK // 128
    return pl.pallas_call(
        _final_norm_kernel,
        grid=(b, SEQ // 128),
        in_specs=[pl.BlockSpec((1, 128, D_MODEL), lambda bi, i: (bi, i + off, 0)),
                  pl.BlockSpec((1, D_MODEL), lambda bi, i: (0, 0))],
        out_specs=pl.BlockSpec((1, 128, D_MODEL), lambda bi, i: (bi, i, 0)),
        out_shape=jax.ShapeDtypeStruct((b, SEQ, D_MODEL), F32),
        name="final_norm",
        compiler_params=pltpu.CompilerParams(
            dimension_semantics=("arbitrary", "arbitrary"), vmem_limit_bytes=VMEM_LIMIT),
    )(hs3d, g.reshape(1, D_MODEL))


def _cast_weights_once(w_refs, wbf_ref):
    @pl.when(pl.program_id(1) == 0)
    def _():
        for seg, w_ref in enumerate(w_refs):
            wbf_ref[seg] = w_ref[...].astype(BF16)


def _proj_attn_kernel(h_ref, wq_ref, wk_ref, wv_ref, wg_ref, q_ref, k_ref, v_ref, g_ref,
                      wbf_ref):
    _cast_weights_once((wq_ref, wk_ref, wv_ref, wg_ref), wbf_ref)
    h = h_ref[...]
    q = jnp.dot(h, wbf_ref[0], preferred_element_type=F32)
    q_ref[...] = (q * (1.0 / math.sqrt(HEAD_DIM))).astype(BF16)
    k_ref[...] = jnp.dot(h, wbf_ref[1], preferred_element_type=F32).astype(BF16)
    v_ref[...] = jnp.dot(h, wbf_ref[2], preferred_element_type=F32).astype(BF16)
    g = jnp.dot(h, wbf_ref[3], preferred_element_type=F32)
    g_ref[...] = g * jax.nn.sigmoid(g)


def _proj_attn(h, w_in, layer, tm):
    m = h.shape[0]
    nj = D_ATTN // PROJ_TN

    def wspec(seg):
        return pl.BlockSpec((None, D_MODEL, PROJ_TN),
                            lambda j, i, seg=seg: (layer, 0, seg * nj + j))

    ospec = pl.BlockSpec((tm, PROJ_TN), lambda j, i: (i, j))
    return pl.pallas_call(
        _proj_attn_kernel,
        grid=(nj, m // tm),
        in_specs=[pl.BlockSpec((tm, D_MODEL), lambda j, i: (i, 0)),
                  wspec(0), wspec(1), wspec(2), wspec(3)],
        out_specs=[ospec, ospec, ospec, ospec],
        out_shape=[jax.ShapeDtypeStruct((m, D_ATTN), BF16)] * 3
        + [jax.ShapeDtypeStruct((m, D_ATTN), F32)],
        scratch_shapes=[pltpu.VMEM((4, D_MODEL, PROJ_TN), BF16)],
        name="proj_attn",
        compiler_params=pltpu.CompilerParams(
            dimension_semantics=("arbitrary", "arbitrary"), vmem_limit_bytes=VMEM_LIMIT),
    )(h, w_in, w_in, w_in, w_in)


def _proj_conv_kernel(h_ref, wb_ref, wc_ref, wh_ref, wz_ref, cw_ref, gain_ref, y_ref,
                      wbf_ref, u_ref):
    i = pl.program_id(1)
    tm = h_ref.shape[0]
    _cast_weights_once((wb_ref, wc_ref, wh_ref, wz_ref), wbf_ref)

    @pl.when(i == 0)
    def _():
        u_ref[0:8, :] = jnp.zeros((8, PROJ_TN), F32)

    @pl.when(i > 0)
    def _():
        u_ref[0:8, :] = u_ref[tm:tm + 8, :]

    cw = cw_ref[...]
    gain = gain_ref[...]
    for r0 in range(0, tm, CONV_CHUNK):
        h = h_ref[r0:r0 + CONV_CHUNK, :]
        u_ref[8 + r0:8 + r0 + CONV_CHUNK, :] = (
            jnp.dot(h, wbf_ref[1], preferred_element_type=F32)
            * jnp.dot(h, wbf_ref[2], preferred_element_type=F32))
        conv = u_ref[6 + r0:6 + r0 + CONV_CHUNK, :] * cw[0:1, :]
        conv = conv + u_ref[7 + r0:7 + r0 + CONV_CHUNK, :] * cw[1:2, :]
        conv = conv + u_ref[8 + r0:8 + r0 + CONV_CHUNK, :] * cw[2:3, :]
        y = jnp.dot(h, wbf_ref[0], preferred_element_type=F32) * conv
        z = jnp.dot(h, wbf_ref[3], preferred_element_type=F32)
        y = y * (z * jax.nn.sigmoid(z))
        for c in range(PROJ_TN // LANES):
            sl = slice(c * LANES, (c + 1) * LANES)
            yc = y[:, sl]
            ms = jnp.sum(yc * yc, axis=-1, keepdims=True) * (1.0 / LANES)
            y_ref[r0:r0 + CONV_CHUNK, sl] = (yc * lax.rsqrt(ms + EPS) * gain[:, sl]).astype(BF16)


def _proj_conv(h, w_in, layer, conv_w_l, conv_gain_l, tm):
    m = h.shape[0]
    nj = D_CONV // PROJ_TN
    base = 4 * D_ATTN // PROJ_TN

    def wspec(seg):
        return pl.BlockSpec((None, D_MODEL, PROJ_TN),
                            lambda j, i, seg=seg: (layer, 0, base + seg * nj + j))

    return pl.pallas_call(
        _proj_conv_kernel,
        grid=(nj, m // tm),
        in_specs=[pl.BlockSpec((tm, D_MODEL), lambda j, i: (i, 0)),
                  wspec(0), wspec(1), wspec(2), wspec(3),
                  pl.BlockSpec((CONV_WIDTH, PROJ_TN), lambda j, i: (0, j)),
                  pl.BlockSpec((1, PROJ_TN), lambda j, i: (0, j))],
        out_specs=pl.BlockSpec((tm, PROJ_TN), lambda j, i: (i, j)),
        out_shape=jax.ShapeDtypeStruct((m, D_CONV), BF16),
        name="proj_conv",
        scratch_shapes=[pltpu.VMEM((4, D_MODEL, PROJ_TN), BF16),
                        pltpu.VMEM((tm + 8, PROJ_TN), F32)],
        compiler_params=pltpu.CompilerParams(
            dimension_semantics=("arbitrary", "arbitrary"), vmem_limit_bytes=VMEM_LIMIT),
    )(h, w_in, w_in, w_in, w_in, conv_w_l, conv_gain_l.reshape(1, D_CONV))


def _strict_lower_neg(n):
    j = lax.broadcasted_iota(jnp.int32, (n, n), 0)
    s = lax.broadcasted_iota(jnp.int32, (n, n), 1)
    return jnp.where(j > s, -1.0, 0.0).astype(BF16)


def _causal(n):
    t = lax.broadcasted_iota(jnp.int32, (n, n), 0)
    s = lax.broadcasted_iota(jnp.int32, (n, n), 1)
    return s < t


def _scores(qt, kb):
    return lax.dot_general(qt, kb, (((1,), (1,)), ((), ())), preferred_element_type=F32)


def _softplus_parts(z, mask):
    sp = jnp.maximum(z, 0.0) + jnp.log(1.0 + jnp.exp(-jnp.abs(z)))
    return z - sp, (sp if mask is None else jnp.where(mask, sp, 0.0))


def _after(sp, tneg):
    return jnp.dot(sp.astype(BF16), tneg, preferred_element_type=F32)


def _pending(log_beta, sp, after, mask):
    e = log_beta + after
    if mask is not None:
        e = jnp.where(mask, e, MASKED_LOG_WEIGHT)
    return e, after[:, 0:1] - sp[:, 0:1]


def _weights(log_w):
    return jnp.exp(log_w.astype(BF16))


def _front(qts, k_ref, heads, rows, tneg, mask):
    s = [_scores(qt, k_ref[rows, hd]) for qt, hd in zip(qts, heads)]
    parts = [_softplus_parts(si, mask) for si in s]
    after = [_after(sp, tneg) for _, sp in parts]
    return [_pending(lb, sp, af, mask) for (lb, sp), af in zip(parts, after)]


def _attn_kernel(q_ref, k_ref, v_ref, g_ref, gain_ref, o_ref, s_ref, acc_ref, run_ref, em_ref):
    tneg_kv = _strict_lower_neg(KV_BLOCK)
    tneg_meta = _strict_lower_neg(META_BLOCK)
    heads = [slice(h * HEAD_DIM, (h + 1) * HEAD_DIM) for h in range(HEADS_PER_STEP)]
    meta_rows = pl.ds(0, META_BLOCK)

    def finish(accs, rows):
        for acc, hd in zip(accs, heads):
            o = acc * g_ref[rows, hd]
            ms = jnp.sum(o * o, axis=-1, keepdims=True) * (1.0 / HEAD_DIM)
            o_ref[rows, hd] = (o * lax.rsqrt(ms + EPS) * gain_ref[:, hd]).astype(o_ref.dtype)

    def weighted_values(ws, rows):
        return [jnp.dot(w, v_ref[rows, hd], preferred_element_type=F32)
                for w, hd in zip(ws, heads)]

    qts = [q_ref[meta_rows, hd] for hd in heads]
    pend = _front(qts, k_ref, heads, meta_rows, tneg_meta, _causal(META_BLOCK))
    finish(weighted_values([_weights(e) for e, _ in pend], meta_rows), meta_rows)

    diag_mask = _causal(KV_BLOCK)
    edge_mask = (lax.broadcasted_iota(jnp.int32, (Q_SUPER, KV_BLOCK), 1)
                 < lax.broadcasted_iota(jnp.int32, (Q_SUPER, KV_BLOCK), 0))
    hp = range(HEADS_PER_STEP)

    def kv_rows(j):
        return pl.ds(pl.multiple_of(META_BLOCK + j * KV_BLOCK, 128), KV_BLOCK)

    def super_tile(t, _):
        r0 = pl.multiple_of(META_BLOCK + t * Q_SUPER, 128)
        rows = pl.ds(r0, Q_SUPER)
        rows_lo = kv_rows(2 * t)
        rows_hi = kv_rows(2 * t + 1)
        q_all = [q_ref[rows, hd] for hd in heads]
        q_hi = [q_ref[rows_hi, hd] for hd in heads]

        s_hi = [_scores(q_hi[h], k_ref[rows_hi, heads[h]]) for h in hp]
        s_lo = [_scores(q_all[h], k_ref[rows_lo, heads[h]]) for h in hp]
        s_me = [_scores(q_all[h], k_ref[meta_rows, heads[h]]) for h in hp]
        first = kv_rows(jnp.maximum(2 * t - 1, 0))
        for h in hp:
            s_ref[h] = _scores(q_all[h], k_ref[first, heads[h]])
        parts_hi = [_softplus_parts(s, diag_mask) for s in s_hi]
        parts_lo = [_softplus_parts(s, edge_mask) for s in s_lo]
        parts_me = [_softplus_parts(s, None) for s in s_me]
        after_hi = [_after(sp, tneg_kv) for _, sp in parts_hi]
        after_lo = [_after(sp, tneg_kv) for _, sp in parts_lo]
        after_me = [_after(sp, tneg_meta) for _, sp in parts_me]
        for h in hp:
            e_hi, tot_hi = _pending(*parts_hi[h], after_hi[h], diag_mask)
            e_lo, tot_lo = _pending(*parts_lo[h], after_lo[h], edge_mask)
            em_ref[h] = _pending(*parts_me[h], after_me[h], None)[0]
            w_hi = _weights(e_hi)
            run = jnp.concatenate([jnp.zeros((KV_BLOCK, 1), F32), tot_hi], axis=0)
            w_lo = _weights(e_lo + run)
            run_ref[h] = jnp.broadcast_to(run + tot_lo, (Q_SUPER, LANES))
            pv_hi = jnp.dot(w_hi, v_ref[rows_hi, heads[h]], preferred_element_type=F32)
            pv_lo = jnp.dot(w_lo, v_ref[rows_lo, heads[h]], preferred_element_type=F32)
            acc_ref[h, 0:KV_BLOCK] = pv_lo[0:KV_BLOCK]
            acc_ref[h, KV_BLOCK:Q_SUPER] = pv_lo[KV_BLOCK:Q_SUPER] + pv_hi

        def add_block(parts, after, j, runs):
            out = []
            for h in hp:
                e, tot = _pending(*parts[h], after[h], None)
                w = _weights(e + jnp.concatenate([runs[h]] * (KV_BLOCK // LANES), axis=1))
                out.append(runs[h] + tot)
                acc_ref[h] += jnp.dot(w, v_ref[kv_rows(j), heads[h]],
                                      preferred_element_type=F32)
            return out

        def sweep(j0, count):
            s = [s_ref[h] for h in hp]
            runs = [run_ref[h] for h in hp]
            for u in range(count):
                parts = [_softplus_parts(si, None) for si in s]
                nxt = kv_rows(jnp.maximum(j0 - u - 1, 0))
                s = [_scores(q_ref[rows, heads[h]], k_ref[nxt, heads[h]]) for h in hp]
                after = [_after(sp, tneg_kv) for _, sp in parts]
                runs = add_block(parts, after, j0 - u, runs)
            for h in hp:
                s_ref[h] = s[h]
                run_ref[h] = runs[h]
            return 0

        lax.fori_loop(0, t % 2, lambda n, c: sweep(2 * t - 1, 2), 0)
        base = 2 * t - 1 - 2 * (t % 2)
        lax.fori_loop(0, t // 2, lambda n, c: sweep(base - LOOP_BLOCKS * n, LOOP_BLOCKS), 0)
        accs = []
        for h in hp:
            w = _weights(em_ref[h] + run_ref[h])
            accs.append(acc_ref[h] + jnp.dot(w, v_ref[meta_rows, heads[h]],
                                             preferred_element_type=F32))
        finish(accs, rows)
        return 0

    lax.fori_loop(0, SEQ // Q_SUPER, super_tile, 0)


def _attention(q, k, v, g, gain_l, batch):
    m = q.shape[0]
    width = HEADS_PER_STEP * HEAD_DIM
    spec = pl.BlockSpec((L_PAD, width), lambda b, h: (b, h))
    return pl.pallas_call(
        _attn_kernel,
        grid=(batch, ATTN_HEADS // HEADS_PER_STEP),
        in_specs=[spec, spec, spec, spec,
                  pl.BlockSpec((1, width), lambda b, h: (0, h))],
        out_specs=spec,
        out_shape=jax.ShapeDtypeStruct((m, D_ATTN), BF16),
        scratch_shapes=[pltpu.VMEM((HEADS_PER_STEP, Q_SUPER, KV_BLOCK), F32),
                        pltpu.VMEM((HEADS_PER_STEP, Q_SUPER, HEAD_DIM), F32),
                        pltpu.VMEM((HEADS_PER_STEP, Q_SUPER, LANES), F32),
                        pltpu.VMEM((HEADS_PER_STEP, Q_SUPER, META_BLOCK), F32)],
        name="sb_attention",
        compiler_params=pltpu.CompilerParams(
            dimension_semantics=("arbitrary", "arbitrary"), vmem_limit_bytes=VMEM_LIMIT),
    )(q, k, v, g, gain_l.reshape(1, D_ATTN))


def _out_proj_kernel(o_ref, y_ref, w_ref, x_ref, out_ref, wbf_ref):
    _cast_weights_once((w_ref,), wbf_ref)
    mix = jnp.concatenate([o_ref[...], y_ref[...]], axis=1)
    out_ref[...] = x_ref[...] + jnp.dot(mix, wbf_ref[0], preferred_element_type=F32)


def _out_proj(o, y, w_out, layer, x2d, tm):
    m = o.shape[0]
    nj = D_MODEL // OUT_TN
    return pl.pallas_call(
        _out_proj_kernel,
        grid=(nj, m // tm),
        in_specs=[pl.BlockSpec((tm, D_ATTN), lambda j, i: (i, 0)),
                  pl.BlockSpec((tm, D_CONV), lambda j, i: (i, 0)),
                  pl.BlockSpec((None, D_MODEL, OUT_TN), lambda j, i: (layer, 0, j)),
                  pl.BlockSpec((tm, OUT_TN), lambda j, i: (i, j))],
        out_specs=pl.BlockSpec((tm, OUT_TN), lambda j, i: (i, j)),
        out_shape=jax.ShapeDtypeStruct((m, D_MODEL), F32),
        scratch_shapes=[pltpu.VMEM((1, D_MODEL, OUT_TN), BF16)],
        name="out_proj",
        compiler_params=pltpu.CompilerParams(
            dimension_semantics=("arbitrary", "arbitrary"), vmem_limit_bytes=VMEM_LIMIT),
    )(o, y, w_out, x2d)


def kernel(x, meta_tokens, norm_g, w_in, conv_w, attn_norm_g, conv_norm_g, w_out, final_norm_g):
    batch = x.shape[0]
    depth = w_in.shape[0]
    hs, h = _embed(x, meta_tokens, norm_g[0])
    assert hs.shape[0] % PROJ_TM == 0 and hs.shape[0] % OUT_TM == 0
    assert PROJ_TM % CONV_CHUNK == 0
    for l in range(depth):
        if l > 0:
            h = _rmsnorm(hs, norm_g[l], BF16, NORM_TM)
        q, k, v, g = _proj_attn(h, w_in, l, PROJ_TM)
        y = _proj_conv(h, w_in, l, conv_w[l], conv_norm_g[l], PROJ_TM)
        o = _attention(q, k, v, g, attn_norm_g[l], batch)
        hs = _out_proj(o, y, w_out, l, hs, OUT_TM)
    return _final_norm(hs.reshape(batch, L_PAD, D_MODEL), final_norm_g)
```

```python
import math

import jax
import jax.numpy as jnp
from jax import lax
from jax.experimental import pallas as pl
from jax.experimental.pallas import tpu as pltpu

D_MODEL = 4096
SEQ = 4096
N_META = 16
HEAD_DIM = 128
ATTN_HEADS = 16
D_ATTN = ATTN_HEADS * HEAD_DIM
D_CONV = D_MODEL - D_ATTN
CONV_WIDTH = 3
EPS = 1e-6

LANES = 128
META_BLOCK = 128
L_PAD = META_BLOCK + SEQ
KV_BLOCK = 256
Q_SUPER = 2 * KV_BLOCK
LOOP_BLOCKS = 4
HEADS_PER_STEP = 2
LOG2_E = 1.4426950408889634
MASKED_LOG_WEIGHT = -1e30
PROJ_TN = 256
PROJ_TM = 704
CONV_CHUNK = 176
OUT_TN = 512
OUT_TM = 1408
NORM_TM = 256
VMEM_LIMIT = 60 * 1024 * 1024

F32 = jnp.float32
BF16 = jnp.bfloat16


def _rmsnorm_kernel(x_ref, g_ref, o_ref):
    x = x_ref[...]
    ms = jnp.sum(x * x, axis=-1, keepdims=True) * (1.0 / D_MODEL)
    o_ref[...] = (x * lax.rsqrt(ms + EPS) * g_ref[...]).astype(o_ref.dtype)


def _rmsnorm(x2d, g, out_dtype, tm):
    m = x2d.shape[0]
    return pl.pallas_call(
        _rmsnorm_kernel,
        grid=(m // tm,),
        in_specs=[pl.BlockSpec((tm, D_MODEL), lambda i: (i, 0)),
                  pl.BlockSpec((1, D_MODEL), lambda i: (0, 0))],
        out_specs=pl.BlockSpec((tm, D_MODEL), lambda i: (i, 0)),
        out_shape=jax.ShapeDtypeStruct((m, D_MODEL), out_dtype),
        name="rmsnorm",
        compiler_params=pltpu.CompilerParams(
            dimension_semantics=("arbitrary",), vmem_limit_bytes=VMEM_LIMIT),
    )(x2d, g.reshape(1, D_MODEL))


def _embed_kernel(x_ref, meta_ref, g_ref, hs_ref, h_ref):
    first = jnp.concatenate(
        [jnp.zeros((META_BLOCK - N_META, D_MODEL), F32), meta_ref[...]], axis=0)
    tile = jnp.where(pl.program_id(1) == 0, first, x_ref[0])
    ms = jnp.sum(tile * tile, axis=-1, keepdims=True) * (1.0 / D_MODEL)
    hs_ref[0] = tile
    h_ref[0] = (tile * lax.rsqrt(ms + EPS) * g_ref[...]).astype(BF16)


def _embed(x, meta_tokens, g):
    b = x.shape[0]
    nblk = L_PAD // META_BLOCK
    blk = pl.BlockSpec((1, META_BLOCK, D_MODEL), lambda bi, i: (bi, i, 0))
    hs, h = pl.pallas_call(
        _embed_kernel,
        grid=(b, nblk),
        in_specs=[pl.BlockSpec((1, META_BLOCK, D_MODEL),
                               lambda bi, i: (bi, jnp.maximum(i - 1, 0), 0)),
                  pl.BlockSpec((N_META, D_MODEL), lambda bi, i: (0, 0)),
                  pl.BlockSpec((1, D_MODEL), lambda bi, i: (0, 0))],
        out_specs=[blk, blk],
        out_shape=[jax.ShapeDtypeStruct((b, L_PAD, D_MODEL), F32),
                   jax.ShapeDtypeStruct((b, L_PAD, D_MODEL), BF16)],
        name="embed_norm",
        compiler_params=pltpu.CompilerParams(
            dimension_semantics=("arbitrary", "arbitrary"), vmem_limit_bytes=VMEM_LIMIT),
    )(x, meta_tokens.astype(x.dtype), g.reshape(1, D_MODEL))
    return hs.reshape(b * L_PAD, D_MODEL), h.reshape(b * L_PAD, D_MODEL)


def _final_norm_kernel(x_ref, g_ref, o_ref):
    x = x_ref[0]
    ms = jnp.sum(x * x, axis=-1, keepdims=True) * (1.0 / D_MODEL)
    o_ref[0] = x * lax.rsqrt(ms + EPS) * g_ref[...]


def _final_norm(hs3d, g):
    b = hs3d.shape[0]
    off = META_BLOCK // 128
    return pl.pallas_call(
        _final_norm_kernel,
        grid=(b, SEQ // 128),
        in_specs=[pl.BlockSpec((1, 128, D_MODEL), lambda bi, i: (bi, i + off, 0)),
                  pl.BlockSpec((1, D_MODEL), lambda bi, i: (0, 0))],
        out_specs=pl.BlockSpec((1, 128, D_MODEL), lambda bi, i: (bi, i, 0)),
        out_shape=jax.ShapeDtypeStruct((b, SEQ, D_MODEL), F32),
        name="final_norm",
        compiler_params=pltpu.CompilerParams(
            dimension_semantics=("arbitrary", "arbitrary"), vmem_limit_bytes=VMEM_LIMIT),
    )(hs3d, g.reshape(1, D_MODEL))


def _cast_weights_once(w_refs, wbf_ref):
    @pl.when(pl.program_id(1) == 0)
    def _():
        for seg, w_ref in enumerate(w_refs):
            wbf_ref[seg] = w_ref[...].astype(BF16)


def _proj_attn_kernel(h_ref, wq_ref, wk_ref, wv_ref, wg_ref, q_ref, k_ref, v_ref, g_ref,
                      wbf_ref):
    _cast_weights_once((wq_ref, wk_ref, wv_ref, wg_ref), wbf_ref)
    h = h_ref[...]
    q = jnp.dot(h, wbf_ref[0], preferred_element_type=F32)
    q_ref[...] = (q * (1.0 / math.sqrt(HEAD_DIM))).astype(BF16)
    k_ref[...] = jnp.dot(h, wbf_ref[1], preferred_element_type=F32).astype(BF16)
    v_ref[...] = jnp.dot(h, wbf_ref[2], preferred_element_type=F32).astype(BF16)
    g = jnp.dot(h, wbf_ref[3], preferred_element_type=F32)
    g_ref[...] = g * jax.nn.sigmoid(g)


def _proj_attn(h, w_in, layer, tm):
    m = h.shape[0]
    nj = D_ATTN // PROJ_TN

    def wspec(seg):
        return pl.BlockSpec((None, D_MODEL, PROJ_TN),
                            lambda j, i, seg=seg: (layer, 0, seg * nj + j))

    ospec = pl.BlockSpec((None, tm, PROJ_TN), lambda j, i: (j, i, 0))
    return pl.pallas_call(
        _proj_attn_kernel,
        grid=(nj, m // tm),
        in_specs=[pl.BlockSpec((tm, D_MODEL), lambda j, i: (i, 0)),
                  wspec(0), wspec(1), wspec(2), wspec(3)],
        out_specs=[ospec, ospec, ospec, ospec],
        out_shape=[jax.ShapeDtypeStruct((nj, m, PROJ_TN), BF16)] * 3
        + [jax.ShapeDtypeStruct((nj, m, PROJ_TN), F32)],
        scratch_shapes=[pltpu.VMEM((4, D_MODEL, PROJ_TN), BF16)],
        name="proj_attn",
        compiler_params=pltpu.CompilerParams(
            dimension_semantics=("arbitrary", "arbitrary"), vmem_limit_bytes=VMEM_LIMIT),
    )(h, w_in, w_in, w_in, w_in)


def _proj_conv_kernel(h_ref, wb_ref, wc_ref, wh_ref, wz_ref, cw_ref, gain_ref, y_ref,
                      wbf_ref, u_ref):
    i = pl.program_id(1)
    tm = h_ref.shape[0]
    _cast_weights_once((wb_ref, wc_ref, wh_ref, wz_ref), wbf_ref)

    @pl.when(i == 0)
    def _():
        u_ref[0:8, :] = jnp.zeros((8, PROJ_TN), F32)

    @pl.when(i > 0)
    def _():
        u_ref[0:8, :] = u_ref[tm:tm + 8, :]

    cw = cw_ref[...]
    gain = gain_ref[...]
    for r0 in range(0, tm, CONV_CHUNK):
        h = h_ref[r0:r0 + CONV_CHUNK, :]
        u_ref[8 + r0:8 + r0 + CONV_CHUNK, :] = (
            jnp.dot(h, wbf_ref[1], preferred_element_type=F32)
            * jnp.dot(h, wbf_ref[2], preferred_element_type=F32))
        conv = u_ref[6 + r0:6 + r0 + CONV_CHUNK, :] * cw[0:1, :]
        conv = conv + u_ref[7 + r0:7 + r0 + CONV_CHUNK, :] * cw[1:2, :]
        conv = conv + u_ref[8 + r0:8 + r0 + CONV_CHUNK, :] * cw[2:3, :]
        y = jnp.dot(h, wbf_ref[0], preferred_element_type=F32) * conv
        z = jnp.dot(h, wbf_ref[3], preferred_element_type=F32)
        y = y * (z * jax.nn.sigmoid(z))
        for c in range(PROJ_TN // LANES):
            sl = slice(c * LANES, (c + 1) * LANES)
            yc = y[:, sl]
            ms = jnp.sum(yc * yc, axis=-1, keepdims=True) * (1.0 / LANES)
            y_ref[r0:r0 + CONV_CHUNK, sl] = (yc * lax.rsqrt(ms + EPS) * gain[:, sl]).astype(BF16)


def _proj_conv(h, w_in, layer, conv_w_l, conv_gain_l, tm):
    m = h.shape[0]
    nj = D_CONV // PROJ_TN
    base = 4 * D_ATTN // PROJ_TN

    def wspec(seg):
        return pl.BlockSpec((None, D_MODEL, PROJ_TN),
                            lambda j, i, seg=seg: (layer, 0, base + seg * nj + j))

    return pl.pallas_call(
        _proj_conv_kernel,
        grid=(nj, m // tm),
        in_specs=[pl.BlockSpec((tm, D_MODEL), lambda j, i: (i, 0)),
                  wspec(0), wspec(1), wspec(2), wspec(3),
                  pl.BlockSpec((CONV_WIDTH, PROJ_TN), lambda j, i: (0, j)),
                  pl.BlockSpec((1, PROJ_TN), lambda j, i: (0, j))],
        out_specs=pl.BlockSpec((tm, PROJ_TN), lambda j, i: (i, j)),
        out_shape=jax.ShapeDtypeStruct((m, D_CONV), BF16),
        name="proj_conv",
        scratch_shapes=[pltpu.VMEM((4, D_MODEL, PROJ_TN), BF16),
                        pltpu.VMEM((tm + 8, PROJ_TN), F32)],
        compiler_params=pltpu.CompilerParams(
            dimension_semantics=("arbitrary", "arbitrary"), vmem_limit_bytes=VMEM_LIMIT),
    )(h, w_in, w_in, w_in, w_in, conv_w_l, conv_gain_l.reshape(1, D_CONV))


def _strict_lower_neg(n):
    j = lax.broadcasted_iota(jnp.int32, (n, n), 0)
    s = lax.broadcasted_iota(jnp.int32, (n, n), 1)
    return jnp.where(j > s, -1.0, 0.0).astype(BF16)


def _causal(n):
    t = lax.broadcasted_iota(jnp.int32, (n, n), 0)
    s = lax.broadcasted_iota(jnp.int32, (n, n), 1)
    return s < t


def _scores(qt, kb):
    return lax.dot_general(qt, kb, (((1,), (1,)), ((), ())), preferred_element_type=F32)


def _softplus_parts(z, mask):
    sp = jnp.maximum(z, 0.0) + jnp.log(1.0 + jnp.exp2(jnp.abs(z) * -LOG2_E))
    return z - sp, (sp if mask is None else jnp.where(mask, sp, 0.0))


def _after(sp, tneg):
    return jnp.dot(sp.astype(BF16), tneg, preferred_element_type=F32)


def _pending(log_beta, sp, after, mask):
    e = log_beta + after
    if mask is not None:
        e = jnp.where(mask, e, MASKED_LOG_WEIGHT)
    return e, after[:, 0:1] - sp[:, 0:1]


def _weights(log_w):
    return jnp.exp(log_w.astype(BF16))


def _front(qts, k_ref, heads, rows, tneg, mask):
    s = [_scores(qt, k_ref[rows, hd]) for qt, hd in zip(qts, heads)]
    parts = [_softplus_parts(si, mask) for si in s]
    after = [_after(sp, tneg) for _, sp in parts]
    return [_pending(lb, sp, af, mask) for (lb, sp), af in zip(parts, after)]


def _attn_kernel(q_ref, k_ref, v_ref, g_ref, gain_ref, o_ref, s_ref, acc_ref, run_ref, em_ref):
    tneg_kv = _strict_lower_neg(KV_BLOCK)
    tneg_meta = _strict_lower_neg(META_BLOCK)
    heads = [slice(h * HEAD_DIM, (h + 1) * HEAD_DIM) for h in range(HEADS_PER_STEP)]
    meta_rows = pl.ds(0, META_BLOCK)

    def finish(accs, rows):
        for acc, hd in zip(accs, heads):
            o = acc * g_ref[rows, hd]
            ms = jnp.sum(o * o, axis=-1, keepdims=True) * (1.0 / HEAD_DIM)
            o_ref[rows, hd] = (o * lax.rsqrt(ms + EPS) * gain_ref[:, hd]).astype(o_ref.dtype)

    def weighted_values(ws, rows):
        return [jnp.dot(w, v_ref[rows, hd], preferred_element_type=F32)
                for w, hd in zip(ws, heads)]

    qts = [q_ref[meta_rows, hd] for hd in heads]
    pend = _front(qts, k_ref, heads, meta_rows, tneg_meta, _causal(META_BLOCK))
    finish(weighted_values([_weights(e) for e, _ in pend], meta_rows), meta_rows)

    diag_mask = _causal(KV_BLOCK)
    edge_mask = (lax.broadcasted_iota(jnp.int32, (Q_SUPER, KV_BLOCK), 1)
                 < lax.broadcasted_iota(jnp.int32, (Q_SUPER, KV_BLOCK), 0))
    hp = range(HEADS_PER_STEP)

    def kv_rows(j):
        return pl.ds(pl.multiple_of(META_BLOCK + j * KV_BLOCK, 128), KV_BLOCK)

    def super_tile(t, _):
        r0 = pl.multiple_of(META_BLOCK + t * Q_SUPER, 128)
        rows = pl.ds(r0, Q_SUPER)
        rows_lo = kv_rows(2 * t)
        rows_hi = kv_rows(2 * t + 1)
        q_all = [q_ref[rows, hd] for hd in heads]
        q_hi = [q_ref[rows_hi, hd] for hd in heads]

        s_hi = [_scores(q_hi[h], k_ref[rows_hi, heads[h]]) for h in hp]
        s_lo = [_scores(q_all[h], k_ref[rows_lo, heads[h]]) for h in hp]
        s_me = [_scores(q_all[h], k_ref[meta_rows, heads[h]]) for h in hp]
        first = kv_rows(jnp.maximum(2 * t - 1, 0))
        for h in hp:
            s_ref[h] = _scores(q_all[h], k_ref[first, heads[h]])
        parts_hi = [_softplus_parts(s, diag_mask) for s in s_hi]
        parts_lo = [_softplus_parts(s, edge_mask) for s in s_lo]
        parts_me = [_softplus_parts(s, None) for s in s_me]
        after_hi = [_after(sp, tneg_kv) for _, sp in parts_hi]
        after_lo = [_after(sp, tneg_kv) for _, sp in parts_lo]
        after_me = [_after(sp, tneg_meta) for _, sp in parts_me]
        for h in hp:
            e_hi, tot_hi = _pending(*parts_hi[h], after_hi[h], diag_mask)
            e_lo, tot_lo = _pending(*parts_lo[h], after_lo[h], edge_mask)
            em_ref[h] = _pending(*parts_me[h], after_me[h], None)[0]
            w_hi = _weights(e_hi)
            run = jnp.concatenate([jnp.zeros((KV_BLOCK, 1), F32), tot_hi], axis=0)
            w_lo = _weights(e_lo + run)
            run_ref[h] = jnp.broadcast_to(run + tot_lo, (Q_SUPER, LANES))
            pv_hi = jnp.dot(w_hi, v_ref[rows_hi, heads[h]], preferred_element_type=F32)
            pv_lo = jnp.dot(w_lo, v_ref[rows_lo, heads[h]], preferred_element_type=F32)
            acc_ref[h, 0:KV_BLOCK] = pv_lo[0:KV_BLOCK]
            acc_ref[h, KV_BLOCK:Q_SUPER] = pv_lo[KV_BLOCK:Q_SUPER] + pv_hi

        def add_block(parts, after, j, runs):
            out = []
            for h in hp:
                e, tot = _pending(*parts[h], after[h], None)
                w = _weights(e + jnp.concatenate([runs[h]] * (KV_BLOCK // LANES), axis=1))
                out.append(runs[h] + tot)
                acc_ref[h] += jnp.dot(w, v_ref[kv_rows(j), heads[h]],
                                      preferred_element_type=F32)
            return out

        def sweep(j0, count):
            s = [s_ref[h] for h in hp]
            runs = [run_ref[h] for h in hp]
            for u in range(count):
                parts = [_softplus_parts(si, None) for si in s]
                nxt = kv_rows(jnp.maximum(j0 - u - 1, 0))
                s = [_scores(q_ref[rows, heads[h]], k_ref[nxt, heads[h]]) for h in hp]
                after = [_after(sp, tneg_kv) for _, sp in parts]
                runs = add_block(parts, after, j0 - u, runs)
            for h in hp:
                s_ref[h] = s[h]
                run_ref[h] = runs[h]
            return 0

        lax.fori_loop(0, t % 2, lambda n, c: sweep(2 * t - 1, 2), 0)
        base = 2 * t - 1 - 2 * (t % 2)
        lax.fori_loop(0, t // 2, lambda n, c: sweep(base - LOOP_BLOCKS * n, LOOP_BLOCKS), 0)
        accs = []
        for h in hp:
            w = _weights(em_ref[h] + run_ref[h])
            accs.append(acc_ref[h] + jnp.dot(w, v_ref[meta_rows, heads[h]],
                                             preferred_element_type=F32))
        finish(accs, rows)
        return 0

    lax.fori_loop(0, SEQ // Q_SUPER, super_tile, 0)


def _attention(q, k, v, g, gain_l, batch):
    m = q.shape[1]
    width = HEADS_PER_STEP * HEAD_DIM
    assert q.shape == (ATTN_HEADS // HEADS_PER_STEP, m, width)
    in_spec = pl.BlockSpec((None, L_PAD, width), lambda b, h: (h, b, 0))
    return pl.pallas_call(
        _attn_kernel,
        grid=(batch, ATTN_HEADS // HEADS_PER_STEP),
        in_specs=[in_spec, in_spec, in_spec, in_spec,
                  pl.BlockSpec((1, width), lambda b, h: (0, h))],
        out_specs=pl.BlockSpec((L_PAD, width), lambda b, h: (b, h)),
        out_shape=jax.ShapeDtypeStruct((m, D_ATTN), BF16),
        scratch_shapes=[pltpu.VMEM((HEADS_PER_STEP, Q_SUPER, KV_BLOCK), F32),
                        pltpu.VMEM((HEADS_PER_STEP, Q_SUPER, HEAD_DIM), F32),
                        pltpu.VMEM((HEADS_PER_STEP, Q_SUPER, LANES), F32),
                        pltpu.VMEM((HEADS_PER_STEP, Q_SUPER, META_BLOCK), F32)],
        name="sb_attention",
        compiler_params=pltpu.CompilerParams(
            dimension_semantics=("arbitrary", "arbitrary"), vmem_limit_bytes=VMEM_LIMIT),
    )(q, k, v, g, gain_l.reshape(1, D_ATTN))


def _out_proj_kernel(o_ref, y_ref, w_ref, x_ref, out_ref, wbf_ref):
    _cast_weights_once((w_ref,), wbf_ref)
    mix = jnp.concatenate([o_ref[...], y_ref[...]], axis=1)
    out_ref[...] = x_ref[...] + jnp.dot(mix, wbf_ref[0], preferred_element_type=F32)


def _out_proj(o, y, w_out, layer, x2d, tm):
    m = o.shape[0]
    nj = D_MODEL // OUT_TN
    return pl.pallas_call(
        _out_proj_kernel,
        grid=(nj, m // tm),
        in_specs=[pl.BlockSpec((tm, D_ATTN), lambda j, i: (i, 0)),
                  pl.BlockSpec((tm, D_CONV), lambda j, i: (i, 0)),
                  pl.BlockSpec((None, D_MODEL, OUT_TN), lambda j, i: (layer, 0, j)),
                  pl.BlockSpec((tm, OUT_TN), lambda j, i: (i, j))],
        out_specs=pl.BlockSpec((tm, OUT_TN), lambda j, i: (i, j)),
        out_shape=jax.ShapeDtypeStruct((m, D_MODEL), F32),
        scratch_shapes=[pltpu.VMEM((1, D_MODEL, OUT_TN), BF16)],
        name="out_proj",
        compiler_params=pltpu.CompilerParams(
            dimension_semantics=("arbitrary", "arbitrary"), vmem_limit_bytes=VMEM_LIMIT),
    )(o, y, w_out, x2d)


def kernel(x, meta_tokens, norm_g, w_in, conv_w, attn_norm_g, conv_norm_g, w_out, final_norm_g):
    batch = x.shape[0]
    depth = w_in.shape[0]
    hs, h = _embed(x, meta_tokens, norm_g[0])
    assert hs.shape[0] % PROJ_TM == 0 and hs.shape[0] % OUT_TM == 0
    assert PROJ_TM % CONV_CHUNK == 0
    for l in range(depth):
        if l > 0:
            h = _rmsnorm(hs, norm_g[l], BF16, NORM_TM)
        q, k, v, g = _proj_attn(h, w_in, l, PROJ_TM)
        y = _proj_conv(h, w_in, l, conv_w[l], conv_norm_g[l], PROJ_TM)
        o = _attention(q, k, v, g, attn_norm_g[l], batch)
        hs = _out_proj(o, y, w_out, l, hs, OUT_TM)
    return _final_norm(hs.reshape(batch, L_PAD, D_MODEL), final_norm_g)
```

```python
import math

import jax
import jax.numpy as jnp
from jax import lax
from jax.experimental import pallas as pl
from jax.experimental.pallas import tpu as pltpu

D_MODEL = 4096
SEQ = 4096
N_META = 16
HEAD_DIM = 128
ATTN_HEADS = 16
D_ATTN = ATTN_HEADS * HEAD_DIM
D_CONV = D_MODEL - D_ATTN
CONV_WIDTH = 3
EPS = 1e-6

LANES = 128
META_BLOCK = 128
L_PAD = META_BLOCK + SEQ
KV_BLOCK = 256
Q_SUPER = 2 * KV_BLOCK
LOOP_BLOCKS = 4
HEADS_PER_STEP = 2
LOG2_E = 1.4426950408889634
MASKED_LOG_WEIGHT = -1e30
PROJ_TN = 256
PROJ_TM = 704
CONV_CHUNK = 176
OUT_TN = 512
OUT_TM = 1408
NORM_TM = 256
VMEM_LIMIT = 60 * 1024 * 1024

F32 = jnp.float32
BF16 = jnp.bfloat16


def _rmsnorm_kernel(x_ref, g_ref, o_ref):
    x = x_ref[...]
    ms = jnp.sum(x * x, axis=-1, keepdims=True) * (1.0 / D_MODEL)
    o_ref[...] = (x * lax.rsqrt(ms + EPS) * g_ref[...]).astype(o_ref.dtype)


def _rmsnorm(x2d, g, out_dtype, tm):
    m = x2d.shape[0]
    return pl.pallas_call(
        _rmsnorm_kernel,
        grid=(m // tm,),
        in_specs=[pl.BlockSpec((tm, D_MODEL), lambda i: (i, 0)),
                  pl.BlockSpec((1, D_MODEL), lambda i: (0, 0))],
        out_specs=pl.BlockSpec((tm, D_MODEL), lambda i: (i, 0)),
        out_shape=jax.ShapeDtypeStruct((m, D_MODEL), out_dtype),
        name="rmsnorm",
        compiler_params=pltpu.CompilerParams(
            dimension_semantics=("arbitrary",), vmem_limit_bytes=VMEM_LIMIT),
    )(x2d, g.reshape(1, D_MODEL))


def _embed_kernel(x_ref, meta_ref, g_ref, hs_ref, h_ref):
    first = jnp.concatenate(
        [jnp.zeros((META_BLOCK - N_META, D_MODEL), F32), meta_ref[...]], axis=0)
    tile = jnp.where(pl.program_id(1) == 0, first, x_ref[0])
    ms = jnp.sum(tile * tile, axis=-1, keepdims=True) * (1.0 / D_MODEL)
    hs_ref[0] = tile
    h_ref[0] = (tile * lax.rsqrt(ms + EPS) * g_ref[...]).astype(BF16)


def _embed(x, meta_tokens, g):
    b = x.shape[0]
    nblk = L_PAD // META_BLOCK
    blk = pl.BlockSpec((1, META_BLOCK, D_MODEL), lambda bi, i: (bi, i, 0))
    hs, h = pl.pallas_call(
        _embed_kernel,
        grid=(b, nblk),
        in_specs=[pl.BlockSpec((1, META_BLOCK, D_MODEL),
                               lambda bi, i: (bi, jnp.maximum(i - 1, 0), 0)),
                  pl.BlockSpec((N_META, D_MODEL), lambda bi, i: (0, 0)),
                  pl.BlockSpec((1, D_MODEL), lambda bi, i: (0, 0))],
        out_specs=[blk, blk],
        out_shape=[jax.ShapeDtypeStruct((b, L_PAD, D_MODEL), F32),
                   jax.ShapeDtypeStruct((b, L_PAD, D_MODEL), BF16)],
        name="embed_norm",
        compiler_params=pltpu.CompilerParams(
            dimension_semantics=("arbitrary", "arbitrary"), vmem_limit_bytes=VMEM_LIMIT),
    )(x, meta_tokens.astype(x.dtype), g.reshape(1, D_MODEL))
    return hs.reshape(b * L_PAD, D_MODEL), h.reshape(b * L_PAD, D_MODEL)


def _final_norm_kernel(x_ref, g_ref, o_ref):
    x = x_ref[0]
    ms = jnp.sum(x * x, axis=-1, keepdims=True) * (1.0 / D_MODEL)
    o_ref[0] = x * lax.rsqrt(ms + EPS) * g_ref[...]


def _final_norm(hs3d, g):
    b = hs3d.shape[0]
    off = META_BLOCK // 128
    return pl.pallas_call(
        _final_norm_kernel,
        grid=(b, SEQ // 128),
        in_specs=[pl.BlockSpec((1, 128, D_MODEL), lambda bi, i: (bi, i + off, 0)),
                  pl.BlockSpec((1, D_MODEL), lambda bi, i: (0, 0))],
        out_specs=pl.BlockSpec((1, 128, D_MODEL), lambda bi, i: (bi, i, 0)),
        out_shape=jax.ShapeDtypeStruct((b, SEQ, D_MODEL), F32),
        name="final_norm",
        compiler_params=pltpu.CompilerParams(
            dimension_semantics=("arbitrary", "arbitrary"), vmem_limit_bytes=VMEM_LIMIT),
    )(hs3d, g.reshape(1, D_MODEL))


def _cast_weights_once(w_refs, wbf_ref):
    @pl.when(pl.program_id(1) == 0)
    def _():
        for seg, w_ref in enumerate(w_refs):
            wbf_ref[seg] = w_ref[...].astype(BF16)


def _proj_attn_kernel(h_ref, wq_ref, wk_ref, wv_ref, wg_ref, q_ref, k_ref, v_ref, g_ref,
                      wbf_ref):
    _cast_weights_once((wq_ref, wk_ref, wv_ref, wg_ref), wbf_ref)
    h = h_ref[...]
    q = jnp.dot(h, wbf_ref[0], preferred_element_type=F32)
    q_ref[...] = (q * (1.0 / math.sqrt(HEAD_DIM))).astype(BF16)
    k_ref[...] = jnp.dot(h, wbf_ref[1], preferred_element_type=F32).astype(BF16)
    v_ref[...] = jnp.dot(h, wbf_ref[2], preferred_element_type=F32).astype(BF16)
    g = jnp.dot(h, wbf_ref[3], preferred_element_type=F32)
    g_ref[...] = g * jax.nn.sigmoid(g)


def _proj_attn(h, w_in, layer, tm):
    m = h.shape[0]
    nj = D_ATTN // PROJ_TN

    def wspec(seg):
        return pl.BlockSpec((None, D_MODEL, PROJ_TN),
                            lambda j, i, seg=seg: (layer, 0, seg * nj + j))

    ospec = pl.BlockSpec((None, tm, PROJ_TN), lambda j, i: (j, i, 0))
    return pl.pallas_call(
        _proj_attn_kernel,
        grid=(nj, m // tm),
        in_specs=[pl.BlockSpec((tm, D_MODEL), lambda j, i: (i, 0)),
                  wspec(0), wspec(1), wspec(2), wspec(3)],
        out_specs=[ospec, ospec, ospec, ospec],
        out_shape=[jax.ShapeDtypeStruct((nj, m, PROJ_TN), BF16)] * 3
        + [jax.ShapeDtypeStruct((nj, m, PROJ_TN), F32)],
        scratch_shapes=[pltpu.VMEM((4, D_MODEL, PROJ_TN), BF16)],
        name="proj_attn",
        compiler_params=pltpu.CompilerParams(
            dimension_semantics=("arbitrary", "arbitrary"), vmem_limit_bytes=VMEM_LIMIT),
    )(h, w_in, w_in, w_in, w_in)


def _proj_conv_kernel(h_ref, wb_ref, wc_ref, wh_ref, wz_ref, cw_ref, gain_ref, y_ref,
                      wbf_ref, u_ref):
    i = pl.program_id(1)
    tm = h_ref.shape[0]
    _cast_weights_once((wb_ref, wc_ref, wh_ref, wz_ref), wbf_ref)

    @pl.when(i == 0)
    def _():
        u_ref[0:8, :] = jnp.zeros((8, PROJ_TN), F32)

    @pl.when(i > 0)
    def _():
        u_ref[0:8, :] = u_ref[tm:tm + 8, :]

    cw = cw_ref[...]
    gain = gain_ref[...]
    for r0 in range(0, tm, CONV_CHUNK):
        h = h_ref[r0:r0 + CONV_CHUNK, :]
        u_ref[8 + r0:8 + r0 + CONV_CHUNK, :] = (
            jnp.dot(h, wbf_ref[1], preferred_element_type=F32)
            * jnp.dot(h, wbf_ref[2], preferred_element_type=F32))
        conv = u_ref[6 + r0:6 + r0 + CONV_CHUNK, :] * cw[0:1, :]
        conv = conv + u_ref[7 + r0:7 + r0 + CONV_CHUNK, :] * cw[1:2, :]
        conv = conv + u_ref[8 + r0:8 + r0 + CONV_CHUNK, :] * cw[2:3, :]
        y = jnp.dot(h, wbf_ref[0], preferred_element_type=F32) * conv
        z = jnp.dot(h, wbf_ref[3], preferred_element_type=F32)
        y = y * (z * jax.nn.sigmoid(z))
        for c in range(PROJ_TN // LANES):
            sl = slice(c * LANES, (c + 1) * LANES)
            yc = y[:, sl]
            ms = jnp.sum(yc * yc, axis=-1, keepdims=True) * (1.0 / LANES)
            y_ref[r0:r0 + CONV_CHUNK, sl] = (yc * lax.rsqrt(ms + EPS) * gain[:, sl]).astype(BF16)


def _proj_conv(h, w_in, layer, conv_w_l, conv_gain_l, tm):
    m = h.shape[0]
    nj = D_CONV // PROJ_TN
    base = 4 * D_ATTN // PROJ_TN

    def wspec(seg):
        return pl.BlockSpec((None, D_MODEL, PROJ_TN),
                            lambda j, i, seg=seg: (layer, 0, base + seg * nj + j))

    return pl.pallas_call(
        _proj_conv_kernel,
        grid=(nj, m // tm),
        in_specs=[pl.BlockSpec((tm, D_MODEL), lambda j, i: (i, 0)),
                  wspec(0), wspec(1), wspec(2), wspec(3),
                  pl.BlockSpec((CONV_WIDTH, PROJ_TN), lambda j, i: (0, j)),
                  pl.BlockSpec((1, PROJ_TN), lambda j, i: (0, j))],
        out_specs=pl.BlockSpec((tm, PROJ_TN), lambda j, i: (i, j)),
        out_shape=jax.ShapeDtypeStruct((m, D_CONV), BF16),
        name="proj_conv",
        scratch_shapes=[pltpu.VMEM((4, D_MODEL, PROJ_TN), BF16),
                        pltpu.VMEM((tm + 8, PROJ_TN), F32)],
        compiler_params=pltpu.CompilerParams(
            dimension_semantics=("arbitrary", "arbitrary"), vmem_limit_bytes=VMEM_LIMIT),
    )(h, w_in, w_in, w_in, w_in, conv_w_l, conv_gain_l.reshape(1, D_CONV))


def _strict_lower_neg(n):
    j = lax.broadcasted_iota(jnp.int32, (n, n), 0)
    s = lax.broadcasted_iota(jnp.int32, (n, n), 1)
    return jnp.where(j > s, -1.0, 0.0).astype(BF16)


def _causal(n):
    t = lax.broadcasted_iota(jnp.int32, (n, n), 0)
    s = lax.broadcasted_iota(jnp.int32, (n, n), 1)
    return s < t


def _scores(qt, kb):
    return lax.dot_general(qt, kb, (((1,), (1,)), ((), ())), preferred_element_type=F32)


def _softplus_parts(z, mask):
    sp = jnp.maximum(z, 0.0) + jnp.log(1.0 + jnp.exp2(jnp.abs(z) * -LOG2_E))
    return z - sp, (sp if mask is None else jnp.where(mask, sp, 0.0))


def _after(sp, tneg):
    return jnp.dot(sp.astype(BF16), tneg, preferred_element_type=F32)


def _pending(log_beta, sp, after, mask):
    e = log_beta + after
    if mask is not None:
        e = jnp.where(mask, e, MASKED_LOG_WEIGHT)
    return e, after[:, 0:1] - sp[:, 0:1]


def _weights(log_w):
    return jnp.exp(log_w.astype(BF16))


def _front(qts, k_ref, heads, rows, tneg, mask):
    s = [_scores(qt, k_ref[rows, hd]) for qt, hd in zip(qts, heads)]
    parts = [_softplus_parts(si, mask) for si in s]
    after = [_after(sp, tneg) for _, sp in parts]
    return [_pending(lb, sp, af, mask) for (lb, sp), af in zip(parts, after)]


def _attn_kernel(q_ref, k_ref, v_ref, g_ref, gain_ref, o_ref, s_ref, acc_ref, run_ref, em_ref):
    tneg_kv = _strict_lower_neg(KV_BLOCK)
    tneg_meta = _strict_lower_neg(META_BLOCK)
    heads = [slice(h * HEAD_DIM, (h + 1) * HEAD_DIM) for h in range(HEADS_PER_STEP)]
    meta_rows = pl.ds(0, META_BLOCK)

    def finish(accs, rows):
        for acc, hd in zip(accs, heads):
            o = acc * g_ref[rows, hd]
            ms = jnp.sum(o * o, axis=-1, keepdims=True) * (1.0 / HEAD_DIM)
            o_ref[rows, hd] = (o * lax.rsqrt(ms + EPS) * gain_ref[:, hd]).astype(o_ref.dtype)

    def weighted_values(ws, rows):
        return [jnp.dot(w, v_ref[rows, hd], preferred_element_type=F32)
                for w, hd in zip(ws, heads)]

    qts = [q_ref[meta_rows, hd] for hd in heads]
    pend = _front(qts, k_ref, heads, meta_rows, tneg_meta, _causal(META_BLOCK))
    finish(weighted_values([_weights(e) for e, _ in pend], meta_rows), meta_rows)

    diag_mask = _causal(KV_BLOCK)
    edge_mask = (lax.broadcasted_iota(jnp.int32, (Q_SUPER, KV_BLOCK), 1)
                 < lax.broadcasted_iota(jnp.int32, (Q_SUPER, KV_BLOCK), 0))
    hp = range(HEADS_PER_STEP)

    def kv_rows(j):
        return pl.ds(pl.multiple_of(META_BLOCK + j * KV_BLOCK, 128), KV_BLOCK)

    def close_tile(t_done):
        rows_done = pl.ds(pl.multiple_of(META_BLOCK + t_done * Q_SUPER, 128), Q_SUPER)
        accs = []
        for h in hp:
            w = _weights(em_ref[h] + run_ref[h])
            accs.append(acc_ref[h] + jnp.dot(w, v_ref[meta_rows, heads[h]],
                                             preferred_element_type=F32))
        finish(accs, rows_done)

    acc_ref[...] = jnp.zeros(acc_ref.shape, F32)
    run_ref[...] = jnp.zeros(run_ref.shape, F32)
    em_ref[...] = jnp.zeros(em_ref.shape, F32)

    def super_tile(t, _):
        close_tile(jnp.maximum(t - 1, 0))
        r0 = pl.multiple_of(META_BLOCK + t * Q_SUPER, 128)
        rows = pl.ds(r0, Q_SUPER)
        rows_lo = kv_rows(2 * t)
        rows_hi = kv_rows(2 * t + 1)
        q_all = [q_ref[rows, hd] for hd in heads]
        q_hi = [q_ref[rows_hi, hd] for hd in heads]

        s_hi = [_scores(q_hi[h], k_ref[rows_hi, heads[h]]) for h in hp]
        s_lo = [_scores(q_all[h], k_ref[rows_lo, heads[h]]) for h in hp]
        s_me = [_scores(q_all[h], k_ref[meta_rows, heads[h]]) for h in hp]
        first = kv_rows(jnp.maximum(2 * t - 1, 0))
        for h in hp:
            s_ref[h] = _scores(q_all[h], k_ref[first, heads[h]])
        parts_hi = [_softplus_parts(s, diag_mask) for s in s_hi]
        parts_lo = [_softplus_parts(s, edge_mask) for s in s_lo]
        parts_me = [_softplus_parts(s, None) for s in s_me]
        after_hi = [_after(sp, tneg_kv) for _, sp in parts_hi]
        after_lo = [_after(sp, tneg_kv) for _, sp in parts_lo]
        after_me = [_after(sp, tneg_meta) for _, sp in parts_me]
        for h in hp:
            e_hi, tot_hi = _pending(*parts_hi[h], after_hi[h], diag_mask)
            e_lo, tot_lo = _pending(*parts_lo[h], after_lo[h], edge_mask)
            em_ref[h] = _pending(*parts_me[h], after_me[h], None)[0]
            w_hi = _weights(e_hi)
            run = jnp.concatenate([jnp.zeros((KV_BLOCK, 1), F32), tot_hi], axis=0)
            w_lo = _weights(e_lo + run)
            run_ref[h] = jnp.broadcast_to(run + tot_lo, (Q_SUPER, LANES))
            pv_hi = jnp.dot(w_hi, v_ref[rows_hi, heads[h]], preferred_element_type=F32)
            pv_lo = jnp.dot(w_lo, v_ref[rows_lo, heads[h]], preferred_element_type=F32)
            acc_ref[h, 0:KV_BLOCK] = pv_lo[0:KV_BLOCK]
            acc_ref[h, KV_BLOCK:Q_SUPER] = pv_lo[KV_BLOCK:Q_SUPER] + pv_hi

        def add_block(parts, after, j, runs):
            out = []
            for h in hp:
                e, tot = _pending(*parts[h], after[h], None)
                w = _weights(e + jnp.concatenate([runs[h]] * (KV_BLOCK // LANES), axis=1))
                out.append(runs[h] + tot)
                acc_ref[h] += jnp.dot(w, v_ref[kv_rows(j), heads[h]],
                                      preferred_element_type=F32)
            return out

        def sweep(j0, count):
            s = [s_ref[h] for h in hp]
            runs = [run_ref[h] for h in hp]
            for u in range(count):
                parts = [_softplus_parts(si, None) for si in s]
                nxt = kv_rows(jnp.maximum(j0 - u - 1, 0))
                s = [_scores(q_ref[rows, heads[h]], k_ref[nxt, heads[h]]) for h in hp]
                after = [_after(sp, tneg_kv) for _, sp in parts]
                runs = add_block(parts, after, j0 - u, runs)
            for h in hp:
                s_ref[h] = s[h]
                run_ref[h] = runs[h]
            return 0

        lax.fori_loop(0, t % 2, lambda n, c: sweep(2 * t - 1, 2), 0)
        base = 2 * t - 1 - 2 * (t % 2)
        lax.fori_loop(0, t // 2, lambda n, c: sweep(base - LOOP_BLOCKS * n, LOOP_BLOCKS), 0)
        return 0

    lax.fori_loop(0, SEQ // Q_SUPER, super_tile, 0)
    close_tile(SEQ // Q_SUPER - 1)


def _attention(q, k, v, g, gain_l, batch):
    m = q.shape[1]
    width = HEADS_PER_STEP * HEAD_DIM
    assert q.shape == (ATTN_HEADS // HEADS_PER_STEP, m, width)
    in_spec = pl.BlockSpec((None, L_PAD, width), lambda b, h: (h, b, 0))
    return pl.pallas_call(
        _attn_kernel,
        grid=(batch, ATTN_HEADS // HEADS_PER_STEP),
        in_specs=[in_spec, in_spec, in_spec, in_spec,
                  pl.BlockSpec((1, width), lambda b, h: (0, h))],
        out_specs=pl.BlockSpec((L_PAD, width), lambda b, h: (b, h)),
        out_shape=jax.ShapeDtypeStruct((m, D_ATTN), BF16),
        scratch_shapes=[pltpu.VMEM((HEADS_PER_STEP, Q_SUPER, KV_BLOCK), F32),
                        pltpu.VMEM((HEADS_PER_STEP, Q_SUPER, HEAD_DIM), F32),
                        pltpu.VMEM((HEADS_PER_STEP, Q_SUPER, LANES), F32),
                        pltpu.VMEM((HEADS_PER_STEP, Q_SUPER, META_BLOCK), F32)],
        name="sb_attention",
        compiler_params=pltpu.CompilerParams(
            dimension_semantics=("arbitrary", "arbitrary"), vmem_limit_bytes=VMEM_LIMIT),
    )(q, k, v, g, gain_l.reshape(1, D_ATTN))


def _out_proj_kernel(o_ref, y_ref, w_ref, x_ref, out_ref, wbf_ref):
    _cast_weights_once((w_ref,), wbf_ref)
    mix = jnp.concatenate([o_ref[...], y_ref[...]], axis=1)
    out_ref[...] = x_ref[...] + jnp.dot(mix, wbf_ref[0], preferred_element_type=F32)


def _out_proj(o, y, w_out, layer, x2d, tm):
    m = o.shape[0]
    nj = D_MODEL // OUT_TN
    return pl.pallas_call(
        _out_proj_kernel,
        grid=(nj, m // tm),
        in_specs=[pl.BlockSpec((tm, D_ATTN), lambda j, i: (i, 0)),
                  pl.BlockSpec((tm, D_CONV), lambda j, i: (i, 0)),
                  pl.BlockSpec((None, D_MODEL, OUT_TN), lambda j, i: (layer, 0, j)),
                  pl.BlockSpec((tm, OUT_TN), lambda j, i: (i, j))],
        out_specs=pl.BlockSpec((tm, OUT_TN), lambda j, i: (i, j)),
        out_shape=jax.ShapeDtypeStruct((m, D_MODEL), F32),
        scratch_shapes=[pltpu.VMEM((1, D_MODEL, OUT_TN), BF16)],
        name="out_proj",
        compiler_params=pltpu.CompilerParams(
            dimension_semantics=("arbitrary", "arbitrary"), vmem_limit_bytes=VMEM_LIMIT),
    )(o, y, w_out, x2d)


def kernel(x, meta_tokens, norm_g, w_in, conv_w, attn_norm_g, conv_norm_g, w_out, final_norm_g):
    batch = x.shape[0]
    depth = w_in.shape[0]
    hs, h = _embed(x, meta_tokens, norm_g[0])
    assert hs.shape[0] % PROJ_TM == 0 and hs.shape[0] % OUT_TM == 0
    assert PROJ_TM % CONV_CHUNK == 0
    for l in range(depth):
        if l > 0:
            h = _rmsnorm(hs, norm_g[l], BF16, NORM_TM)
        q, k, v, g = _proj_attn(h, w_in, l, PROJ_TM)
        y = _proj_conv(h, w_in, l, conv_w[l], conv_norm_g[l], PROJ_TM)
        o = _attention(q, k, v, g, attn_norm_g[l], batch)
        hs = _out_proj(o, y, w_out, l, hs, OUT_TM)
    return _final_norm(hs.reshape(batch, L_PAD, D_MODEL), final_norm_g)
```
